```python
import functools
import jax, jax.numpy as jnp
from jax import lax
import numpy as np

D_MODEL = 1024
BATCH = 16
SEQ = 2048
DEPTH = 2
DEC_BATCH = 32
DEC_SEQ = 1
PAST_LEN = 16384
PAGE_SIZE = 128

N_HEADS = 8
HEAD_DIM = D_MODEL // N_HEADS
N_KV_HEADS = 4
N_IDX_HEADS = 8
IDX_DIM = 64
INDEX_SCALE = (IDX_DIM * N_IDX_HEADS) ** -0.5
TOPK_MAX = 256
Q_BLOCK = 128
ROPE_THETA = 10000.0
D_RNN = D_MODEL
N_LRU_BLOCKS = 8
LRU_BLOCK = D_RNN // N_LRU_BLOCKS
CONV_WIDTH = 4
LRU_C = 8.0
D_POOL = D_MODEL
POOL_WINDOWS = (2, 4, 8, 16)
POOL_GROUP = D_POOL // len(POOL_WINDOWS)
POOL_BUF = max(POOL_WINDOWS) - 1
N_BRANCHES = 3
N_EXPERT_GROUPS = 4
EXPERTS_PER_GROUP = 8
N_EXPERTS = N_EXPERT_GROUPS * EXPERTS_PER_GROUP
TOP_K_IN_GROUP = 2
D_EXPERT = 512
EXPERT_BLOCK = 128
PLE_DIM = 256
LN_EPS = 1e-5
DEEPNORM_ALPHA = (2 * DEPTH) ** 0.25
DEEPNORM_BETA = (8 * DEPTH) ** -0.25
IN_SPLITS = (N_HEADS * HEAD_DIM, N_KV_HEADS * HEAD_DIM, N_KV_HEADS * HEAD_DIM, N_IDX_HEADS * IDX_DIM,
             N_IDX_HEADS, IDX_DIM, D_RNN, D_RNN, D_POOL, N_BRANCHES * D_MODEL)
D_IN = sum(IN_SPLITS)

kernel_name = 'hawk_dsa_pool_hmoe_decoder_step'


def layer_norm(x, g, b):
    xf = x.astype(jnp.float32)
    mu = jnp.mean(xf, -1, keepdims=True)
    var = jnp.mean(jnp.square(xf - mu), -1, keepdims=True)
    return ((xf - mu) * lax.rsqrt(var + LN_EPS) * g + b).astype(x.dtype)


def rope(x, pos):
    half = x.shape[-1] // 2
    freq = ROPE_THETA ** (-jnp.arange(half, dtype=jnp.float32) / half)
    ang = pos.astype(jnp.float32)[:, None] * freq[None, :]
    cos = jnp.cos(ang)[:, None, :]
    sin = jnp.sin(ang)[:, None, :]
    xf = x.astype(jnp.float32)
    x1, x2 = xf[..., :half], xf[..., half:]
    return jnp.concatenate([x1 * cos - x2 * sin, x2 * cos + x1 * sin], -1).astype(x.dtype)


take_rows = jax.vmap(lambda a, i: a[i])


def index_scores(iq, iw, ik):
    s = jnp.einsum('bqhd,bsd->bqhs', iq, ik).astype(jnp.float32)
    return jnp.einsum('bqhs,bqh->bqs', jax.nn.relu(s), iw.astype(jnp.float32) * INDEX_SCALE)


def sparse_attend(q, k_sel, v_sel, valid):
    b, nq = q.shape[:2]
    qg = q.reshape(b, nq, N_KV_HEADS, N_HEADS // N_KV_HEADS, HEAD_DIM)
    s = jnp.einsum('bqcgd,bqkcd->bqcgk', qg, k_sel).astype(jnp.float32) * (HEAD_DIM ** -0.5)
    s = jnp.where(valid[:, :, None, None, :], s, -jnp.inf)
    p = jax.nn.softmax(s, axis=-1).astype(v_sel.dtype)
    o = jnp.einsum('bqcgk,bqkcd->bqcgd', p, v_sel)
    return o.reshape(b, nq, N_HEADS * HEAD_DIM)


def dsa_prompt(q, k, v, iq, iw, ik):
    b, t = q.shape[:2]
    topk = min(TOPK_MAX, t // 4)
    nblk = t // Q_BLOCK
    key_pos = jnp.arange(t)

    def one_block(args):
        qb, iqb, iwb, start = args
        tq = start + jnp.arange(Q_BLOCK)
        sc = index_scores(iqb, iwb, ik)
        sc = jnp.where((key_pos[None, :] <= tq[:, None])[None], sc, -jnp.inf)
        _, idx = lax.top_k(sc, topk)
        valid = idx <= tq[None, :, None]
        return sparse_attend(qb, take_rows(k, idx), take_rows(v, idx), valid)

    def to_blocks(a):
        return jnp.moveaxis(a.reshape(b, nblk, Q_BLOCK, *a.shape[2:]), 1, 0)

    out = lax.map(one_block, (to_blocks(q), to_blocks(iq), to_blocks(iw), jnp.arange(nblk) * Q_BLOCK))
    return jnp.moveaxis(out, 0, 1).reshape(b, t, N_HEADS * HEAD_DIM)


def dsa_sample(q, k, v, iq, iw, ik, cache_k, cache_v, cache_idx, page_table, layer):
    bd, tn = q.shape[:2]
    past = page_table.shape[1] * PAGE_SIZE
    total = past + tn
    topk = min(TOPK_MAX, total // 4)
    ik_past = cache_idx[layer, page_table].reshape(bd, past, IDX_DIM).astype(ik.dtype)
    ik_all = jnp.concatenate([ik_past, ik], 1)
    tq = past + jnp.arange(tn)
    sc = index_scores(iq, iw, ik_all)
    sc = jnp.where((jnp.arange(total)[None, :] <= tq[:, None])[None], sc, -jnp.inf)
    _, idx = lax.top_k(sc, topk)
    valid = idx <= tq[None, :, None]
    pidx = jnp.minimum(idx, past - 1)
    phys = take_rows(page_table, pidx // PAGE_SIZE)
    off = pidx % PAGE_SIZE
    nidx = jnp.clip(idx - past, 0, tn - 1)
    from_past = (idx < past)[..., None, None]
    k_sel = jnp.where(from_past, cache_k[layer, phys, off].astype(k.dtype), take_rows(k, nidx))
    v_sel = jnp.where(from_past, cache_v[layer, phys, off].astype(v.dtype), take_rows(v, nidx))
    return sparse_attend(q, k_sel, v_sel, valid)


def rglru_branch(xr, gr, conv_prev, h_prev, conv_w, conv_b, wa, ba, wi, bi, lam):
    b, t, _ = xr.shape
    xp = jnp.concatenate([conv_prev.astype(xr.dtype), xr], 1)
    xc = conv_b + sum(conv_w[j] * xp[:, j:j + t] for j in range(CONV_WIDTH))
    blk = xc.reshape(b, t, N_LRU_BLOCKS, LRU_BLOCK)
    r = jax.nn.sigmoid((jnp.einsum('btnc,ncd->btnd', blk, wa) + ba).astype(jnp.float32)).reshape(b, t, D_RNN)
    i = jax.nn.sigmoid((jnp.einsum('btnc,ncd->btnd', blk, wi) + bi).astype(jnp.float32)).reshape(b, t, D_RNN)
    log_a = -LRU_C * r * jax.nn.softplus(-lam.astype(jnp.float32))
    a = jnp.exp(log_a)
    u = jnp.sqrt(-jnp.expm1(2.0 * log_a)) * i * xc.astype(jnp.float32)

    def step(h, au):
        h = au[0] * h + au[1]
        return h, h

    h_last, hs = lax.scan(step, h_prev.astype(jnp.float32), (jnp.moveaxis(a, 1, 0), jnp.moveaxis(u, 1, 0)))
    y = jnp.moveaxis(hs, 0, 1).astype(xr.dtype) * jax.nn.gelu(gr)
    return y, h_last.astype(xr.dtype), xp[:, t:]


def pool_branch(xq, prev, pos, pool_w, pool_scale):
    b, t, _ = xq.shape
    xcat = jnp.concatenate([prev.astype(xq.dtype), xq], 1)
    cs = jnp.cumsum(xcat.astype(jnp.float32), axis=1)
    cs = jnp.concatenate([jnp.zeros((b, 1, D_POOL), jnp.float32), cs], 1)
    outs = []
    for g, w in enumerate(POOL_WINDOWS):
        sl = slice(g * POOL_GROUP, (g + 1) * POOL_GROUP)
        wsum = cs[:, POOL_BUF + 1:POOL_BUF + 1 + t, sl] - cs[:, POOL_BUF + 1 - w:POOL_BUF + 1 - w + t, sl]
        cnt = jnp.minimum(pos + 1, w).astype(jnp.float32)
        outs.append(wsum / cnt[None, :, None])
    pooled = (jnp.concatenate(outs, -1) - xq.astype(jnp.float32)).astype(xq.dtype)
    mixed = jnp.einsum('btgc,gcd->btgd', pooled.reshape(b, t, len(POOL_WINDOWS), POOL_GROUP), pool_w)
    return mixed.reshape(b, t, D_POOL) * pool_scale, xcat[:, t:]


def mixer_sublayer(x, pos, conv_prev, h_prev, pool_prev, attend, w_in, w_out, conv_w, conv_b,
                   wa, ba, wi, bi, lam, pool_w, pool_scale):
    b, t, _ = x.shape
    z = x @ w_in
    cuts, acc = [], 0
    for s in IN_SPLITS[:-1]:
        acc += s
        cuts.append(acc)
    q, k, v, iq, iw, ik, xr, gr, xq, gz = jnp.split(z, cuts, axis=-1)
    q = rope(q.reshape(b, t, N_HEADS, HEAD_DIM), pos)
    k = rope(k.reshape(b, t, N_KV_HEADS, HEAD_DIM), pos)
    v = v.reshape(b, t, N_KV_HEADS, HEAD_DIM)
    iq = rope(iq.reshape(b, t, N_IDX_HEADS, IDX_DIM), pos)
    ik = rope(ik[:, :, None, :], pos)[:, :, 0]
    y_att = attend(q, k, v, iq, iw, ik)
    y_lru, h_new, conv_new = rglru_branch(xr, gr, conv_prev, h_prev, conv_w, conv_b, wa, ba, wi, bi, lam)
    y_pool, pool_new = pool_branch(xq, pool_prev, pos, pool_w, pool_scale)
    g = jax.nn.sigmoid(gz.reshape(b, t, N_BRANCHES, D_MODEL))
    merged = g[:, :, 0] * y_att + g[:, :, 1] * y_lru + g[:, :, 2] * y_pool
    return merged @ w_out, k, v, ik, h_new, conv_new, pool_new


def routed_experts(xt, eidx, wts, w_gate, w_up, w_down):
    n, d = xt.shape
    na = eidx.size
    flat_e = eidx.reshape(na)
    flat_tok = jnp.repeat(jnp.arange(n, dtype=jnp.int32), TOP_K_IN_GROUP)
    flat_w = wts.reshape(na)
    order = jnp.argsort(flat_e)
    e_sorted = flat_e[order]
    counts = jnp.bincount(flat_e, length=N_EXPERTS)
    padded = (counts + EXPERT_BLOCK - 1) // EXPERT_BLOCK * EXPERT_BLOCK
    pad_end = jnp.cumsum(padded)
    pad_start = pad_end - padded
    start = jnp.cumsum(counts) - counts
    dest = pad_start[e_sorted] + jnp.arange(na) - start[e_sorted]
    n_blocks = -(-na // EXPERT_BLOCK) + N_EXPERTS
    slot_tok = jnp.full((n_blocks * EXPERT_BLOCK,), n, jnp.int32).at[dest].set(flat_tok[order])
    block_e = jnp.minimum(jnp.searchsorted(pad_end, jnp.arange(n_blocks) * EXPERT_BLOCK, side='right'), N_EXPERTS - 1)
    x_pad = jnp.concatenate([xt, jnp.zeros((1, d), xt.dtype)], 0)
    xb = x_pad[slot_tok].reshape(n_blocks, EXPERT_BLOCK, d)

    def run(args):
        xblk, e = args
        h = jax.nn.silu(xblk @ w_gate[e]) * (xblk @ w_up[e])
        return h @ w_down[e]

    yb = lax.map(run, (xb, block_e)).reshape(n_blocks * EXPERT_BLOCK, d)
    y_assign = (yb[dest] * flat_w[order][:, None]).astype(xt.dtype)
    return jax.ops.segment_sum(y_assign, flat_tok[order], num_segments=n)


def hier_moe(x, w_rg, b_rg, w_re, b_re, w_gate, w_up, w_down):
    b, t, d = x.shape
    xt = x.reshape(b * t, d)
    pg = jax.nn.softmax((xt @ w_rg).astype(jnp.float32) + b_rg, axis=-1)
    p_top, g = lax.top_k(pg, 1)
    le = ((xt @ w_re).astype(jnp.float32) + b_re).reshape(b * t, N_EXPERT_GROUPS, EXPERTS_PER_GROUP)
    le_g = jnp.take_along_axis(le, g[:, :, None], axis=1)[:, 0]
    wtop, ltop = lax.top_k(jax.nn.softmax(le_g, axis=-1), TOP_K_IN_GROUP)
    wtop = wtop / jnp.sum(wtop, -1, keepdims=True) * p_top
    eidx = g * EXPERTS_PER_GROUP + ltop
    return routed_experts(xt, eidx, wtop, w_gate, w_up, w_down).reshape(b, t, d)


def channel_sublayer(x, p, w_rg, b_rg, w_re, b_re, w_gate, w_up, w_down, w_ple, w_ple_gate, g, b):
    y = hier_moe(x, w_rg, b_rg, w_re, b_re, w_gate, w_up, w_down)
    y = y + jax.nn.sigmoid(x @ w_ple_gate) * (p.astype(x.dtype) @ w_ple)
    return layer_norm(DEEPNORM_ALPHA * x + y, g, b)


def setup_inputs(seed: int = 0) -> dict:
    key = jax.random.key(seed)
    ks = iter(jax.random.split(key, 48))
    f32 = jnp.float32
    n_pages = PAST_LEN // PAGE_SIZE
    n_used = DEC_BATCH * n_pages
    n_pool = n_used + max(1, n_used // 4)

    def nrm(shape, scale):
        return scale * jax.random.normal(next(ks), shape, f32)

    x_prompt = nrm((BATCH, SEQ, D_MODEL), 1.0)
    x_sample = nrm((DEC_BATCH, DEC_SEQ, D_MODEL), 1.0)
    p_prompt = nrm((DEPTH, BATCH, SEQ, PLE_DIM), 1.0)
    p_sample = nrm((DEPTH, DEC_BATCH, DEC_SEQ, PLE_DIM), 1.0)
    cache_k = nrm((DEPTH, n_pool, PAGE_SIZE, N_KV_HEADS, HEAD_DIM), 1.0)
    cache_v = nrm((DEPTH, n_pool, PAGE_SIZE, N_KV_HEADS, HEAD_DIM), 1.0)
    cache_idx = nrm((DEPTH, n_pool, PAGE_SIZE, IDX_DIM), 1.0)
    state_lru_h = nrm((DEPTH, DEC_BATCH, D_RNN), 0.5)
    state_lru_conv = nrm((DEPTH, DEC_BATCH, CONV_WIDTH - 1, D_RNN), 1.0)
    state_pool = nrm((DEPTH, DEC_BATCH, POOL_BUF, D_POOL), 1.0)
    page_table = jax.random.permutation(next(ks), n_pool)[:n_used].reshape(DEC_BATCH, n_pages).astype(jnp.int32)
    u = jax.random.uniform(next(ks), (DEPTH, D_RNN), f32, 0.9, 0.999)
    s = u ** (1.0 / LRU_C)
    lru_lambda = jnp.log(s) - jnp.log1p(-s)
    return {
        'x_prompt': x_prompt, 'x_sample': x_sample, 'p_prompt': p_prompt, 'p_sample': p_sample,
        'cache_k': cache_k, 'cache_v': cache_v, 'cache_idx': cache_idx,
        'state_lru_h': state_lru_h, 'state_lru_conv': state_lru_conv, 'state_pool': state_pool,
        'page_table': page_table,
        'w_in': nrm((DEPTH, D_MODEL, D_IN), D_MODEL ** -0.5),
        'w_out': nrm((DEPTH, D_MODEL, D_MODEL), DEEPNORM_BETA * D_MODEL ** -0.5),
        'lru_conv_w': nrm((DEPTH, CONV_WIDTH, D_RNN), CONV_WIDTH ** -0.5),
        'lru_conv_b': nrm((DEPTH, D_RNN), 0.05),
        'lru_wa': nrm((DEPTH, N_LRU_BLOCKS, LRU_BLOCK, LRU_BLOCK), LRU_BLOCK ** -0.5),
        'lru_ba': nrm((DEPTH, N_LRU_BLOCKS, LRU_BLOCK), 0.1),
        'lru_wi': nrm((DEPTH, N_LRU_BLOCKS, LRU_BLOCK, LRU_BLOCK), LRU_BLOCK ** -0.5),
        'lru_bi': nrm((DEPTH, N_LRU_BLOCKS, LRU_BLOCK), 0.1),
        'lru_lambda': lru_lambda,
        'pool_w': nrm((DEPTH, len(POOL_WINDOWS), POOL_GROUP, POOL_GROUP), POOL_GROUP ** -0.5),
        'pool_scale': 1.0 + nrm((DEPTH, D_POOL), 0.1),
        'ln1_g': 1.0 + nrm((DEPTH, D_MODEL), 0.05),
        'ln1_b': nrm((DEPTH, D_MODEL), 0.05),
        'w_router_group': nrm((DEPTH, D_MODEL, N_EXPERT_GROUPS), D_MODEL ** -0.5),
        'b_router_group': nrm((DEPTH, N_EXPERT_GROUPS), 0.01),
        'w_router_expert': nrm((DEPTH, D_MODEL, N_EXPERTS), D_MODEL ** -0.5),
        'b_router_expert': nrm((DEPTH, N_EXPERTS), 0.01),
        'w_exp_gate': nrm((DEPTH, N_EXPERTS, D_MODEL, D_EXPERT), D_MODEL ** -0.5),
        'w_exp_up': nrm((DEPTH, N_EXPERTS, D_MODEL, D_EXPERT), D_MODEL ** -0.5),
        'w_exp_down': nrm((DEPTH, N_EXPERTS, D_EXPERT, D_MODEL), DEEPNORM_BETA * D_EXPERT ** -0.5),
        'w_ple': nrm((DEPTH, PLE_DIM, D_MODEL), DEEPNORM_BETA * PLE_DIM ** -0.5),
        'w_ple_gate': nrm((DEPTH, D_MODEL, D_MODEL), D_MODEL ** -0.5),
        'ln2_g': 1.0 + nrm((DEPTH, D_MODEL), 0.05),
        'ln2_b': nrm((DEPTH, D_MODEL), 0.05),
    }


def reference(x_prompt, x_sample, p_prompt, p_sample, cache_k, cache_v, cache_idx,
              state_lru_h, state_lru_conv, state_pool, page_table,
              w_in, w_out, lru_conv_w, lru_conv_b, lru_wa, lru_ba, lru_wi, lru_bi, lru_lambda,
              pool_w, pool_scale, ln1_g, ln1_b, w_router_group, b_router_group,
              w_router_expert, b_router_expert, w_exp_gate, w_exp_up, w_exp_down,
              w_ple, w_ple_gate, ln2_g, ln2_b):
    b_p, t_p = x_prompt.shape[:2]
    b_s, t_s = x_sample.shape[:2]
    past = page_table.shape[1] * PAGE_SIZE
    pos_p = jnp.arange(t_p)
    pos_s = past + jnp.arange(t_s)
    dt = x_prompt.dtype
    conv0 = jnp.zeros((b_p, CONV_WIDTH - 1, D_RNN), dt)
    h0 = jnp.zeros((b_p, D_RNN), dt)
    pool0 = jnp.zeros((b_p, POOL_BUF, D_POOL), dt)
    xp, xs = x_prompt, x_sample
    kp_l, vp_l, ip_l, hp_l, cp_l, pp_l = [], [], [], [], [], []
    ks_l, vs_l, is_l, hs_l, cs_l, ps_l = [], [], [], [], [], []
    for i in range(DEPTH):
        mix_w = (w_in[i], w_out[i], lru_conv_w[i], lru_conv_b[i], lru_wa[i], lru_ba[i],
                 lru_wi[i], lru_bi[i], lru_lambda[i], pool_w[i], pool_scale[i])
        ffn_w = (w_router_group[i], b_router_group[i], w_router_expert[i], b_router_expert[i],
                 w_exp_gate[i], w_exp_up[i], w_exp_down[i], w_ple[i], w_ple_gate[i], ln2_g[i], ln2_b[i])
        attend_s = functools.partial(dsa_sample, cache_k=cache_k, cache_v=cache_v, cache_idx=cache_idx,
                                     page_table=page_table, layer=i)
        mp, kp, vp, ikp, hlp, cvp, plp = mixer_sublayer(xp, pos_p, conv0, h0, pool0, dsa_prompt, *mix_w)
        ms, ks, vs, iks, hls, cvs, pls = mixer_sublayer(xs, pos_s, state_lru_conv[i], state_lru_h[i],
                                                         state_pool[i], attend_s, *mix_w)
        xp = layer_norm(DEEPNORM_ALPHA * xp + mp, ln1_g[i], ln1_b[i])
        xs = layer_norm(DEEPNORM_ALPHA * xs + ms, ln1_g[i], ln1_b[i])
        xp = channel_sublayer(xp, p_prompt[i], *ffn_w)
        xs = channel_sublayer(xs, p_sample[i], *ffn_w)
        kp_l.append(kp); vp_l.append(vp); ip_l.append(ikp); hp_l.append(hlp); cp_l.append(cvp); pp_l.append(plp)
        ks_l.append(ks); vs_l.append(vs); is_l.append(iks); hs_l.append(hls); cs_l.append(cvs); ps_l.append(pls)
    return (xp, xs,
            jnp.stack(kp_l), jnp.stack(vp_l), jnp.stack(ip_l), jnp.stack(hp_l), jnp.stack(cp_l), jnp.stack(pp_l),
            jnp.stack(ks_l), jnp.stack(vs_l), jnp.stack(is_l), jnp.stack(hs_l), jnp.stack(cs_l), jnp.stack(ps_l))
```

```python
import functools

import jax
import jax.numpy as jnp
import numpy as np
from jax import lax
from jax.experimental import pallas as pl
from jax.experimental.pallas import tpu as pltpu

F32 = jnp.float32
BF16 = jnp.bfloat16
I32 = jnp.int32

D_MODEL = 1024
N_HEADS = 8
HEAD_DIM = 128
N_KV_HEADS = 4
KV_GROUP = N_HEADS // N_KV_HEADS
N_IDX_HEADS = 8
IDX_DIM = 64
INDEX_SCALE = (IDX_DIM * N_IDX_HEADS) ** -0.5
TOPK_MAX = 256
ROPE_THETA = 10000.0
PAGE_SIZE = 128
N_LRU_BLOCKS = 8
LRU_BLOCK = D_MODEL // N_LRU_BLOCKS
CONV_WIDTH = 4
LRU_C = 8.0
POOL_WINDOWS = (2, 4, 8, 16)
POOL_GROUP = D_MODEL // len(POOL_WINDOWS)
POOL_BUF = max(POOL_WINDOWS) - 1
N_EXPERT_GROUPS = 4
EXPERTS_PER_GROUP = 8
N_EXPERTS = N_EXPERT_GROUPS * EXPERTS_PER_GROUP
D_EXPERT = 512
PLE_DIM = 256
LN_EPS = 1e-5
DEPTH = 2
DEEPNORM_ALPHA = (2 * DEPTH) ** 0.25
ATTN_SCALE = HEAD_DIM ** -0.5

LANES = 128
SUBLANES = 8
SLOT_BLOCK = 256
VMEM_LIMIT = 48 * 1024 * 1024
INT_MIN = -2 ** 31
KEY_NEG_INF = INT_MIN + 0x7FFFFF

_NT = (((1,), (1,)), ((), ()))


def _dot(a, b):
    return jnp.dot(a, b, preferred_element_type=F32)


def _dot_nt(a, b):
    return lax.dot_general(a, b, _NT, preferred_element_type=F32)


def _sigmoid(x):
    return 1.0 / (1.0 + jnp.exp(-x))


def _gelu_tanh(x):
    c = np.float32(np.sqrt(2.0 / np.pi))
    return x * (0.5 * (1.0 + jnp.tanh(c * (x + 0.044715 * (x * x * x)))))


def _softplus(x):
    return jnp.maximum(x, 0.0) + jnp.log1p(jnp.exp(-jnp.abs(x)))


def _layer_norm(r, g, b):
    mu = jnp.mean(r, axis=-1, keepdims=True)
    c = r - mu
    var = jnp.mean(c * c, axis=-1, keepdims=True)
    return c * lax.rsqrt(var + LN_EPS) * g + b


def _sort_key(x):
    bits = pltpu.bitcast(jnp.where(x == 0.0, 0.0, x), I32)
    return jnp.where(bits >= 0, bits, bits ^ 0x7FFFFFFF)


def _params(*sem):
    return pltpu.CompilerParams(dimension_semantics=sem, vmem_limit_bytes=VMEM_LIMIT)


def _full(shape):
    n = len(shape)
    return pl.BlockSpec(shape, lambda *_: (0,) * n)


def _proj_attn_kernel(x_ref, wq, wk, wv, wiq, wiw, wik, cos_ref, sin_ref, cosi_ref, sini_ref,
                      q_o, k_o, v_o, kb_o, vb_o, iq_o, iw_o, ik_o):
    xb = x_ref[...].astype(BF16)
    cos, sin = cos_ref[...], sin_ref[...]
    cosi, sini = cosi_ref[...], sini_ref[...]
    lane = lax.broadcasted_iota(I32, cos.shape, 1)
    low_half = (lane & (IDX_DIM - 1)) < (IDX_DIM // 2)

    def rope128(z):
        return z * cos + pltpu.roll(z, HEAD_DIM // 2, 1) * sin

    def rope64(z):
        partner = jnp.where(low_half, pltpu.roll(z, LANES - IDX_DIM // 2, 1), pltpu.roll(z, IDX_DIM // 2, 1))
        return z * cosi + partner * sini

    q = _dot(xb, wq[...])
    for h in range(N_HEADS):
        sl = slice(h * LANES, (h + 1) * LANES)
        q_o[:, sl] = (rope128(q[:, sl]) * ATTN_SCALE).astype(BF16)
    k = _dot(xb, wk[...])
    for c in range(N_KV_HEADS):
        sl = slice(c * LANES, (c + 1) * LANES)
        kr = rope128(k[:, sl])
        k_o[:, sl] = kr
        kb_o[:, sl] = kr.astype(BF16)
    v = _dot(xb, wv[...])
    v_o[...] = v
    vb_o[...] = v.astype(BF16)
    iq = _dot(xb, wiq[...])
    for j in range(N_IDX_HEADS * IDX_DIM // LANES):
        sl = slice(j * LANES, (j + 1) * LANES)
        iq_o[:, sl] = rope64(iq[:, sl])
    iw_o[...] = _dot(xb, wiw[...])
    ik_o[...] = rope64(_dot(xb, wik[...]))


def _proj_attn(x, w, tabs, tm, n_tab_blocks):
    n = x.shape[0]
    cos, sin, cosi, sini = tabs
    row = lambda i: (i, 0)
    tab = lambda i: (i % n_tab_blocks, 0)
    wspec = lambda a: _full(a.shape)
    outs = [
        ((n, 1024), BF16), ((n, 512), F32), ((n, 512), F32), ((n, 512), BF16), ((n, 512), BF16),
        ((n, 512), F32), ((n, LANES), F32), ((n, LANES), F32)]
    return pl.pallas_call(
        _proj_attn_kernel,
        grid=(n // tm,),
        in_specs=[pl.BlockSpec((tm, D_MODEL), row)] + [wspec(a) for a in w]
        + [pl.BlockSpec((tm, LANES), tab)] * 4,
        out_specs=[pl.BlockSpec((tm, s[1]), row) for s, _ in outs],
        out_shape=[jax.ShapeDtypeStruct(s, d) for s, d in outs],
        compiler_params=_params("parallel"),
        name="proj_attn",
    )(x, *w, cos, sin, cosi, sini)


def _count_rows(mask):
    return jnp.sum(jnp.where(mask, 1.0, 0.0), axis=1, keepdims=True)


def _dsa_prompt_kernel(q_ref, kb_ref, vb_ref, iq_ref, iw_ref, ik_ref, o_ref, p_sc, *, tq, t_len, topk, idx_bits):
    qi = pl.program_id(1)
    ikb = ik_ref[...].astype(BF16)
    iq = iq_ref[...]
    iw = iw_ref[...] * INDEX_SCALE
    lane = lax.broadcasted_iota(I32, (tq, LANES), 1)
    score = jnp.zeros((tq, t_len), F32)
    for h in range(N_IDX_HEADS):
        chunk = iq[:, (h // 2) * LANES:(h // 2 + 1) * LANES]
        keep = (lane < IDX_DIM) if h % 2 == 0 else (lane >= IDX_DIM)
        s = _dot_nt(jnp.where(keep, chunk, 0.0).astype(BF16), ikb)
        score = score + jnp.maximum(s, 0.0) * iw[:, h:h + 1]
    qpos = qi * tq + lax.broadcasted_iota(I32, (tq, 1), 0)
    kpos = lax.broadcasted_iota(I32, (tq, t_len), 1)
    causal = kpos <= qpos
    key = _sort_key(jnp.where(causal, score, -jnp.inf))

    def thr_body(b, t):
        cand = t + lax.shift_left(jnp.int32(1), 31 - b)
        return jnp.where(_count_rows(key >= cand) >= topk, cand, t)

    thr = lax.fori_loop(0, 32, thr_body, jnp.full((tq, 1), INT_MIN, I32))
    gt = key > thr
    eq = key == thr
    need = topk - _count_rows(gt)
    tie = (_count_rows(eq) > need) & (thr > KEY_NEG_INF)
    p_sc[...] = jnp.full((tq, 1), t_len, I32)

    @pl.when(jnp.max(jnp.where(tie, 1.0, 0.0)) > 0.0)
    def _():
        def pos_body(b, p):
            cand = p + lax.shift_left(jnp.int32(1), idx_bits - 1 - b)
            return jnp.where(_count_rows(eq & (kpos < cand)) < need, cand, p)
        p_sc[...] = lax.fori_loop(0, idx_bits, pos_body, jnp.zeros((tq, 1), I32))

    sel = causal & (gt | (eq & (kpos <= p_sc[...])))
    bias = jnp.where(sel, 0.0, -jnp.inf)
    for c in range(N_KV_HEADS):
        kc = kb_ref[:, c * LANES:(c + 1) * LANES]
        vc = vb_ref[:, c * LANES:(c + 1) * LANES]
        for g in range(KV_GROUP):
            h = c * KV_GROUP + g
            s = _dot_nt(q_ref[:, h * LANES:(h + 1) * LANES], kc) + bias
            m = jnp.max(s, axis=1, keepdims=True)
            p = jnp.exp(s - m)
            l = jnp.sum(p, axis=1, keepdims=True)
            o_ref[:, h * LANES:(h + 1) * LANES] = _dot(p.astype(BF16), vc) / l


def _dsa_prompt(q, kb, vb, iq, iw, ik, batch, t_len, tq):
    topk = min(TOPK_MAX, t_len // 4)
    nq = t_len // tq
    idx_bits = max(1, int(np.ceil(np.log2(t_len))))
    qrow = lambda b, i: (b * nq + i, 0)
    brow = lambda b, i: (b, 0)
    return pl.pallas_call(
        functools.partial(_dsa_prompt_kernel, tq=tq, t_len=t_len, topk=topk, idx_bits=idx_bits),
        grid=(batch, nq),
        in_specs=[pl.BlockSpec((tq, 1024), qrow), pl.BlockSpec((t_len, 512), brow), pl.BlockSpec((t_len, 512), brow),
                  pl.BlockSpec((tq, 512), qrow), pl.BlockSpec((tq, LANES), qrow), pl.BlockSpec((t_len, LANES), brow)],
        out_specs=pl.BlockSpec((tq, 1024), qrow),
        out_shape=jax.ShapeDtypeStruct((batch * t_len, 1024), F32),
        scratch_shapes=[pltpu.VMEM((tq, 1), I32)],
        compiler_params=_params("parallel", "arbitrary"),
        name="dsa_prompt",
    )(q, kb, vb, iq, iw, ik)


def _dsa_sample_select_kernel(pt_ref, iq_ref, iw_ref, ikn_ref, cidx_ref, idx_o,
                              ikbuf, s_sc, c_sc, sem, *, layer, n_pages, topk, idx_bits):
    b = pl.program_id(0)
    past = n_pages * PAGE_SIZE

    def page_copy(j):
        return pltpu.make_async_copy(cidx_ref.at[layer, pt_ref[b * n_pages + j]],
                                     ikbuf.at[pl.ds(j * PAGE_SIZE, PAGE_SIZE)], sem)

    def issue(j, c):
        page_copy(j).start()
        return c

    def drain(j, c):
        page_copy(j).wait()
        return c

    lax.fori_loop(0, n_pages, issue, 0)
    lax.fori_loop(0, n_pages, drain, 0)

    iq8 = iq_ref[...].astype(BF16)
    w8 = iw_ref[...] * INDEX_SCALE
    sc8 = _dot_nt(iq8, ikbuf[...].astype(BF16))
    sc = jnp.sum(jnp.maximum(sc8, 0.0) * w8, axis=0, keepdims=True)
    for j in range(n_pages):
        s_sc[j:j + 1, :] = sc[:, j * PAGE_SIZE:(j + 1) * PAGE_SIZE]
    own = jnp.sum(iq8.astype(F32) * ikn_ref[...].astype(BF16).astype(F32), axis=1, keepdims=True)
    own = jnp.sum(jnp.maximum(own, 0.0) * w8, axis=0, keepdims=True)

    key = _sort_key(s_sc[...])
    key_own = _sort_key(own)
    pos = (lax.broadcasted_iota(I32, key.shape, 0) * PAGE_SIZE + lax.broadcasted_iota(I32, key.shape, 1))

    def count(mask, mask_own):
        c = jnp.sum(jnp.where(mask, 1.0, 0.0), axis=0, keepdims=True)
        return jnp.sum(c, axis=1, keepdims=True) + jnp.where(mask_own, 1.0, 0.0)

    def thr_body(i, t):
        cand = t + lax.shift_left(jnp.int32(1), 31 - i)
        return jnp.where(count(key >= cand, key_own >= cand) >= topk, cand, t)

    thr = lax.fori_loop(0, 32, thr_body, jnp.full((1, 1), INT_MIN, I32))
    gt, eq = key > thr, key == thr
    gt_own, eq_own = key_own > thr, key_own == thr
    need = topk - count(gt, gt_own)

    def pos_body(i, p):
        cand = p + lax.shift_left(jnp.int32(1), idx_bits - 1 - i)
        return jnp.where(count(eq & (pos < cand), eq_own & (past < cand)) < need, cand, p)

    plast = lax.fori_loop(0, idx_bits, pos_body, jnp.zeros((1, 1), I32))
    sel = gt | (eq & (pos <= plast))

    r_i = lax.broadcasted_iota(I32, (PAGE_SIZE, PAGE_SIZE), 0)
    c_i = lax.broadcasted_iota(I32, (PAGE_SIZE, PAGE_SIZE), 1)
    within = _dot(jnp.where(sel, 1.0, 0.0).astype(BF16), jnp.where(r_i <= c_i, 1.0, 0.0).astype(BF16))
    tot = jnp.broadcast_to(within[:, PAGE_SIZE - 1:PAGE_SIZE], (n_pages, PAGE_SIZE)).astype(BF16)
    pr = lax.broadcasted_iota(I32, (n_pages, n_pages), 0)
    pc = lax.broadcasted_iota(I32, (n_pages, n_pages), 1)
    c_sc[...] = within + _dot(jnp.where(pc < pr, 1.0, 0.0).astype(BF16), tot)

    rank = lax.broadcasted_iota(I32, (topk, PAGE_SIZE), 0).astype(F32)

    def acc_body(j, acc):
        return acc + jnp.where(c_sc[pl.ds(j, 1), :] <= rank, 1.0, 0.0)

    acc = lax.fori_loop(0, n_pages, acc_body, jnp.zeros((topk, PAGE_SIZE), F32))
    idx_o[...] = jnp.sum(acc, axis=1, keepdims=True).astype(I32)


def _dsa_sample_select(page_table, iq, iw, ik_new, cache_idx, layer):
    bd, n_pages = page_table.shape
    past = n_pages * PAGE_SIZE
    topk = min(TOPK_MAX, (past + 1) // 4)
    idx_bits = int(np.floor(np.log2(past))) + 1
    grid_spec = pltpu.PrefetchScalarGridSpec(
        num_scalar_prefetch=1,
        grid=(bd,),
        in_specs=[pl.BlockSpec((None, N_IDX_HEADS, IDX_DIM), lambda b, pt: (b, 0, 0)),
                  pl.BlockSpec((None, N_IDX_HEADS, 1), lambda b, pt: (b, 0, 0)),
                  pl.BlockSpec((None, 1, IDX_DIM), lambda b, pt: (b, 0, 0)),
                  pl.BlockSpec(memory_space=pl.ANY)],
        out_specs=pl.BlockSpec((None, topk, 1), lambda b, pt: (b, 0, 0)),
        scratch_shapes=[pltpu.VMEM((past, IDX_DIM), F32), pltpu.VMEM((n_pages, PAGE_SIZE), F32),
                        pltpu.VMEM((n_pages, PAGE_SIZE), F32), pltpu.SemaphoreType.DMA])
    return pl.pallas_call(
        functools.partial(_dsa_sample_select_kernel, layer=layer, n_pages=n_pages, topk=topk, idx_bits=idx_bits),
        grid_spec=grid_spec,
        out_shape=jax.ShapeDtypeStruct((bd, topk, 1), I32),
        compiler_params=_params("arbitrary"),
        name="dsa_sample_select",
    )(page_table.reshape(-1), iq.reshape(bd, N_IDX_HEADS, IDX_DIM), iw[:, :N_IDX_HEADS, None],
      ik_new[:, None, :], cache_idx)


def _dsa_sample_attend_kernel(idx_ref, pt_ref, q_ref, kn_ref, vn_ref, idxv_ref, ck_ref, cv_ref, o_ref,
                              kbuf, vbuf, sem, *, layer, n_pages, topk):
    b = pl.program_id(0)
    past = n_pages * PAGE_SIZE

    def row_copies(r):
        pidx = jnp.minimum(idx_ref[b * topk + r], past - 1)
        phys = pt_ref[b * n_pages + pidx // PAGE_SIZE]
        off = pidx % PAGE_SIZE
        return (pltpu.make_async_copy(ck_ref.at[layer, phys, off], kbuf.at[r], sem.at[0]),
                pltpu.make_async_copy(cv_ref.at[layer, phys, off], vbuf.at[r], sem.at[1]))

    def issue(r, c):
        for cp in row_copies(r):
            cp.start()
        return c

    def drain(r, c):
        for cp in row_copies(r):
            cp.wait()
        return c

    lax.fori_loop(0, topk, issue, 0)
    lax.fori_loop(0, topk, drain, 0)

    own = idxv_ref[...] >= past
    k_sel = jnp.where(own, kn_ref[...][None], kbuf[...])
    v_sel = jnp.where(own, vn_ref[...][None], vbuf[...])
    for g in range(KV_GROUP):
        s = jnp.sum(k_sel * q_ref[g][None], axis=-1, keepdims=True)
        m = jnp.max(s, axis=0, keepdims=True)
        p = jnp.exp(s - m)
        l = jnp.sum(p, axis=0)
        o_ref[g] = jnp.sum(p * v_sel, axis=0) / l


def _dsa_sample_attend(idx, page_table, q, k_new, v_new, cache_k, cache_v, layer):
    bd, n_pages = page_table.shape
    topk = idx.shape[1]
    qg = q.astype(F32).reshape(bd, N_KV_HEADS, KV_GROUP, HEAD_DIM).transpose(0, 2, 1, 3)
    head = lambda b, *_: (b, 0, 0)
    grid_spec = pltpu.PrefetchScalarGridSpec(
        num_scalar_prefetch=2,
        grid=(bd,),
        in_specs=[pl.BlockSpec((None, KV_GROUP, N_KV_HEADS, HEAD_DIM), lambda b, *_: (b, 0, 0, 0)),
                  pl.BlockSpec((None, N_KV_HEADS, HEAD_DIM), head),
                  pl.BlockSpec((None, N_KV_HEADS, HEAD_DIM), head),
                  pl.BlockSpec((None, topk, 1, 1), lambda b, *_: (b, 0, 0, 0)),
                  pl.BlockSpec(memory_space=pl.ANY), pl.BlockSpec(memory_space=pl.ANY)],
        out_specs=pl.BlockSpec((None, KV_GROUP, N_KV_HEADS, HEAD_DIM), lambda b, *_: (b, 0, 0, 0)),
        scratch_shapes=[pltpu.VMEM((topk, N_KV_HEADS, HEAD_DIM), F32), pltpu.VMEM((topk, N_KV_HEADS, HEAD_DIM), F32),
                        pltpu.SemaphoreType.DMA((2,))])
    o = pl.pallas_call(
        functools.partial(_dsa_sample_attend_kernel, layer=layer, n_pages=n_pages, topk=topk),
        grid_spec=grid_spec,
        out_shape=jax.ShapeDtypeStruct((bd, KV_GROUP, N_KV_HEADS, HEAD_DIM), F32),
        compiler_params=_params("arbitrary"),
        name="dsa_sample_attend",
    )(idx.reshape(-1), page_table.reshape(-1), qg, k_new.reshape(bd, N_KV_HEADS, HEAD_DIM),
      v_new.reshape(bd, N_KV_HEADS, HEAD_DIM), idx.reshape(bd, topk, 1, 1), cache_k, cache_v)
    return o.transpose(0, 2, 1, 3).reshape(bd, N_HEADS * HEAD_DIM)


def _lru_gates(xc, wa, ba, wi, bi, lam):
    xcb = xc.astype(BF16)
    a_parts, u_parts = [], []
    for n in range(N_LRU_BLOCKS):
        sl = slice(n * LRU_BLOCK, (n + 1) * LRU_BLOCK)
        r = _sigmoid(_dot(xcb[:, sl], wa[n]) + ba[:, sl])
        i = _sigmoid(_dot(xcb[:, sl], wi[n]) + bi[:, sl])
        log_a = -LRU_C * r * _softplus(-lam[:, sl])
        a_parts.append(jnp.exp(log_a))
        th = jnp.tanh(log_a)
        u_parts.append(jnp.sqrt(-2.0 * th / (1.0 - th)) * i * xc[:, sl])
    return a_parts, u_parts


def _pool_mix(window_sum, xq, cnt, pool_w, pool_scale, g):
    sl = slice(g * POOL_GROUP, (g + 1) * POOL_GROUP)
    pooled = window_sum / cnt - xq[:, sl]
    return _dot(pooled.astype(BF16), pool_w[g]) * pool_scale[:, sl]


def _mix_prompt_kernel(x_ref, wxr, wgr, wxq, convw, convb, wa, ba, wi, bi, lam, poolw, pscale,
                       ylru_o, ypool_o, h_o, conv_o, pool_o,
                       xr_ext, xq_ext, a0, u0, a1, u1, h_sc, *, tt, pad):
    t = pl.program_id(1)
    halo_r, halo_q = SUBLANES, 2 * SUBLANES

    @pl.when(t == 0)
    def _():
        xr_ext[0:halo_r] = jnp.zeros((halo_r, D_MODEL), F32)
        xq_ext[0:halo_q] = jnp.zeros((halo_q, D_MODEL), F32)
        h_sc[...] = jnp.zeros_like(h_sc)
        a0[0:pad] = jnp.ones((pad, D_MODEL), F32)
        a1[0:pad] = jnp.ones((pad, D_MODEL), F32)
        u0[0:pad] = jnp.zeros((pad, D_MODEL), F32)
        u1[0:pad] = jnp.zeros((pad, D_MODEL), F32)

    @pl.when(t > 0)
    def _():
        xr_ext[0:halo_r] = xr_ext[tt:tt + halo_r]
        xq_ext[0:halo_q] = xq_ext[tt:tt + halo_q]

    xb = x_ref[...].astype(BF16)
    xr = _dot(xb, wxr[...])
    xq = _dot(xb, wxq[...])
    xr_ext[halo_r:halo_r + tt] = xr
    xq_ext[halo_q:halo_q + tt] = xq

    cw = convw[...]
    xc = convb[...] + cw[CONV_WIDTH - 1:CONV_WIDTH] * xr
    for j in range(CONV_WIDTH - 1):
        o = halo_r - (CONV_WIDTH - 1) + j
        xc = xc + cw[j:j + 1] * xr_ext[o:o + tt]
    a_parts, u_parts = _lru_gates(xc, wa, ba[...], wi, bi[...], lam[...])
    for n in range(N_LRU_BLOCKS):
        sl = slice(n * LRU_BLOCK, (n + 1) * LRU_BLOCK)
        a0[pad:pad + tt, sl] = a_parts[n]
        u0[pad:pad + tt, sl] = u_parts[n]

    bufs = ((a0, u0), (a1, u1))
    d, cur = 1, 0
    while d < tt:
        (sa, su), (da, du) = bufs[cur], bufs[1 - cur]
        a_cur, u_cur = sa[pad:pad + tt], su[pad:pad + tt]
        du[pad:pad + tt] = u_cur + a_cur * su[pad - d:pad - d + tt]
        da[pad:pad + tt] = a_cur * sa[pad - d:pad - d + tt]
        d, cur = 2 * d, 1 - cur
    sa, su = bufs[cur]
    h = sa[pad:pad + tt] * h_sc[...] + su[pad:pad + tt]
    h_sc[...] = h[tt - 1:tt]
    ylru_o[...] = h * _gelu_tanh(_dot(xb, wgr[...]))

    posn = t * tt + lax.broadcasted_iota(I32, (tt, 1), 0)
    for g, w in enumerate(POOL_WINDOWS):
        sl = slice(g * POOL_GROUP, (g + 1) * POOL_GROUP)
        acc = xq[:, sl]
        for j in range(1, w):
            acc = acc + xq_ext[halo_q - j:halo_q - j + tt, sl]
        cnt = jnp.minimum(posn + 1, w).astype(F32)
        ypool_o[:, sl] = _pool_mix(acc, xq, cnt, poolw, pscale[...], g)

    @pl.when(t == pl.num_programs(1) - 1)
    def _():
        h_o[...] = h[tt - 1:tt]
        conv_o[...] = xr[tt - halo_r:tt]
        pool_o[...] = xq[tt - halo_q:tt]


def _mix_prompt(x, w, batch, t_len, tt):
    nt = t_len // tt
    pad = tt // 2
    row = lambda b, t: (b * nt + t, 0)
    per_b = lambda b, t: (b, 0, 0)
    outs = [((batch * t_len, D_MODEL), F32), ((batch * t_len, D_MODEL), F32),
            ((batch, 1, D_MODEL), F32), ((batch, SUBLANES, D_MODEL), F32), ((batch, 2 * SUBLANES, D_MODEL), F32)]
    return pl.pallas_call(
        functools.partial(_mix_prompt_kernel, tt=tt, pad=pad),
        grid=(batch, nt),
        in_specs=[pl.BlockSpec((tt, D_MODEL), row)] + [_full(a.shape) for a in w],
        out_specs=[pl.BlockSpec((tt, D_MODEL), row), pl.BlockSpec((tt, D_MODEL), row),
                   pl.BlockSpec((None, 1, D_MODEL), per_b), pl.BlockSpec((None, SUBLANES, D_MODEL), per_b),
                   pl.BlockSpec((None, 2 * SUBLANES, D_MODEL), per_b)],
        out_shape=[jax.ShapeDtypeStruct(s, d) for s, d in outs],
        scratch_shapes=[pltpu.VMEM((tt + SUBLANES, D_MODEL), F32), pltpu.VMEM((tt + 2 * SUBLANES, D_MODEL), F32)]
        + [pltpu.VMEM((pad + tt, D_MODEL), F32)] * 4 + [pltpu.VMEM((1, D_MODEL), F32)],
        compiler_params=_params("parallel", "arbitrary"),
        name="mix_prompt",
    )(x, *w)


def _mix_sample_kernel(x_ref, wxr, wgr, wxq, convw, convb, wa, ba, wi, bi, lam, poolw, pscale,
                       conv_ref, h_ref, pool_ref, ylru_o, ypool_o, h_o, xr_o, xq_o, *, cnt_pos):
    xb = x_ref[...].astype(BF16)
    xr = _dot(xb, wxr[...])
    xq = _dot(xb, wxq[...])
    cw = convw[...]
    xc = convb[...] + cw[CONV_WIDTH - 1:CONV_WIDTH] * xr
    for j in range(CONV_WIDTH - 1):
        xc = xc + cw[j:j + 1] * conv_ref[j]
    a_parts, u_parts = _lru_gates(xc, wa, ba[...], wi, bi[...], lam[...])
    gate = _gelu_tanh(_dot(xb, wgr[...]))
    for n in range(N_LRU_BLOCKS):
        sl = slice(n * LRU_BLOCK, (n + 1) * LRU_BLOCK)
        h = a_parts[n] * h_ref[:, sl] + u_parts[n]
        h_o[:, sl] = h
        ylru_o[:, sl] = h * gate[:, sl]
    for g, w in enumerate(POOL_WINDOWS):
        sl = slice(g * POOL_GROUP, (g + 1) * POOL_GROUP)
        acc = xq[:, sl]
        for j in range(1, w):
            acc = acc + pool_ref[POOL_BUF - j, :, sl]
        ypool_o[:, sl] = _pool_mix(acc, xq, float(min(cnt_pos, w)), poolw, pscale[...], g)
    xr_o[...] = xr
    xq_o[...] = xq


def _mix_sample(x, w, conv_state, h_state, pool_state, past):
    bd = x.shape[0]
    args = (x, *w, conv_state.transpose(1, 0, 2), h_state, pool_state.transpose(1, 0, 2))
    shp = jax.ShapeDtypeStruct((bd, D_MODEL), F32)
    return pl.pallas_call(
        functools.partial(_mix_sample_kernel, cnt_pos=past + 1),
        in_specs=[_full(a.shape) for a in args],
        out_specs=[_full((bd, D_MODEL))] * 5,
        out_shape=[shp] * 5,
        grid=(1,),
        compiler_params=_params("arbitrary"),
        name="mix_sample",
    )(*args)


def _merge_kernel(x_ref, ya_ref, yl_ref, yp_ref, wgz, wout, g_ref, b_ref, x1_o):
    x = x_ref[...]
    gz = _dot(x.astype(BF16), wgz[...])
    merged = (_sigmoid(gz[:, 0:D_MODEL]) * ya_ref[...] + _sigmoid(gz[:, D_MODEL:2 * D_MODEL]) * yl_ref[...]
              + _sigmoid(gz[:, 2 * D_MODEL:3 * D_MODEL]) * yp_ref[...])
    r = DEEPNORM_ALPHA * x + _dot(merged.astype(BF16), wout[...])
    x1_o[...] = _layer_norm(r, g_ref[...], b_ref[...])


def _merge(x, ya, yl, yp, wgz, wout, g, b, tm):
    n = x.shape[0]
    row = pl.BlockSpec((tm, D_MODEL), lambda i: (i, 0))
    return pl.pallas_call(
        _merge_kernel,
        grid=(n // tm,),
        in_specs=[row] * 4 + [_full(wgz.shape), _full(wout.shape), _full(g.shape), _full(b.shape)],
        out_specs=row,
        out_shape=jax.ShapeDtypeStruct((n, D_MODEL), F32),
        compiler_params=_params("parallel"),
        name="merge",
    )(x, ya, yl, yp, wgz, wout, g, b)


ROUTER_ROWS = 40


def _first_argmax(v, n):
    m = jnp.max(v, axis=0, keepdims=True)
    rows = lax.broadcasted_iota(I32, v.shape, 0)
    return m, jnp.min(jnp.where(v == m, rows, n), axis=0, keepdims=True)


def _router_kernel(x_ref, w_ref, b_ref, ei_o, wt_o, cnt_o, carry, *, tm):
    i = pl.program_id(0)

    @pl.when(i == 0)
    def _():
        carry[...] = jnp.zeros_like(carry)

    def split(v):
        hi = v.astype(BF16)
        return hi, (v - hi.astype(F32)).astype(BF16)

    x_hi, x_lo = split(x_ref[...])
    w_hi, w_lo = split(w_ref[...])
    logits = _dot_nt(w_hi, x_hi) + (_dot_nt(w_hi, x_lo) + _dot_nt(w_lo, x_hi)) + b_ref[...]
    le = logits[0:N_EXPERTS]
    lg = logits[N_EXPERTS:N_EXPERTS + N_EXPERT_GROUPS]

    gmax, gidx = _first_argmax(lg, N_EXPERT_GROUPS)
    p_top = 1.0 / jnp.sum(jnp.exp(lg - gmax), axis=0, keepdims=True)
    le_g = jnp.zeros((EXPERTS_PER_GROUP, tm), F32)
    for gi in range(N_EXPERT_GROUPS):
        le_g = le_g + jnp.where(gidx == gi, le[gi * EXPERTS_PER_GROUP:(gi + 1) * EXPERTS_PER_GROUP], 0.0)
    m1, i1 = _first_argmax(le_g, EXPERTS_PER_GROUP)
    rows8 = lax.broadcasted_iota(I32, le_g.shape, 0)
    m2, i2 = _first_argmax(jnp.where(rows8 == i1, -jnp.inf, le_g), EXPERTS_PER_GROUP)
    z = jnp.sum(jnp.exp(le_g - m1), axis=0, keepdims=True)
    p1 = 1.0 / z
    p2 = jnp.exp(m2 - m1) / z
    e1 = gidx * EXPERTS_PER_GROUP + i1
    e2 = gidx * EXPERTS_PER_GROUP + i2

    rows = lax.broadcasted_iota(I32, (N_EXPERTS, tm), 0)
    hit1, hit2 = rows == e1, rows == e2
    onehot = jnp.where(hit1 | hit2, 1.0, 0.0)
    r_i = lax.broadcasted_iota(I32, (tm, tm), 0)
    c_i = lax.broadcasted_iota(I32, (tm, tm), 1)
    before = _dot(onehot.astype(BF16), jnp.where(r_i < c_i, 1.0, 0.0).astype(BF16)) + carry[...]
    carry[...] = carry[...] + jnp.sum(onehot, axis=1, keepdims=True)

    ei_o[0:1, :] = e1
    ei_o[1:2, :] = e2
    ei_o[2:3, :] = jnp.sum(jnp.where(hit1, before, 0.0), axis=0, keepdims=True).astype(I32)
    ei_o[3:4, :] = jnp.sum(jnp.where(hit2, before, 0.0), axis=0, keepdims=True).astype(I32)
    ei_o[4:SUBLANES, :] = jnp.zeros((SUBLANES - 4, tm), I32)
    wt_o[0:1, :] = p1 / (p1 + p2) * p_top
    wt_o[1:2, :] = p2 / (p1 + p2) * p_top
    wt_o[2:SUBLANES, :] = jnp.zeros((SUBLANES - 2, tm), F32)
    cnt_o[...] = carry[...]


def _router(x1, w_rt, b_rt, tm):
    n = x1.shape[0]
    col = pl.BlockSpec((SUBLANES, tm), lambda i: (0, i))
    return pl.pallas_call(
        functools.partial(_router_kernel, tm=tm),
        grid=(n // tm,),
        in_specs=[pl.BlockSpec((tm, D_MODEL), lambda i: (i, 0)), _full(w_rt.shape), _full(b_rt.shape)],
        out_specs=[col, col, _full((N_EXPERTS, 1))],
        out_shape=[jax.ShapeDtypeStruct((SUBLANES, n), I32), jax.ShapeDtypeStruct((SUBLANES, n), F32),
                   jax.ShapeDtypeStruct((N_EXPERTS, 1), F32)],
        scratch_shapes=[pltpu.VMEM((N_EXPERTS, 1), F32)],
        compiler_params=_params("arbitrary"),
        name="router",
    )(x1, w_rt, b_rt)


def _dispatch_kernel(dest_ref, x_ref, xs_in, xs_out, sem, *, tm):
    del xs_in
    i = pl.program_id(0)

    def row_copy(r, k):
        return pltpu.make_async_copy(x_ref.at[pl.ds(r, 1)], xs_out.at[pl.ds(dest_ref[2 * (i * tm + r) + k], 1)], sem)

    def issue(r, c):
        row_copy(r, 0).start()
        row_copy(r, 1).start()
        return c

    def drain(r, c):
        row_copy(r, 0).wait()
        row_copy(r, 1).wait()
        return c

    lax.fori_loop(0, tm, issue, 0)
    lax.fori_loop(0, tm, drain, 0)


def _dispatch(dest, x1, n_slots, tm):
    n = x1.shape[0]
    grid_spec = pltpu.PrefetchScalarGridSpec(
        num_scalar_prefetch=1,
        grid=(n // tm,),
        in_specs=[pl.BlockSpec((tm, D_MODEL), lambda i, d: (i, 0)), pl.BlockSpec(memory_space=pl.ANY)],
        out_specs=pl.BlockSpec(memory_space=pl.ANY),
        scratch_shapes=[pltpu.SemaphoreType.DMA])
    return pl.pallas_call(
        functools.partial(_dispatch_kernel, tm=tm),
        grid_spec=grid_spec,
        out_shape=jax.ShapeDtypeStruct((n_slots, D_MODEL), F32),
        input_output_aliases={2: 0},
        compiler_params=_params("arbitrary"),
        name="dispatch",
    )(dest, x1, jnp.zeros((n_slots, D_MODEL), F32))


def _expert_kernel(be_ref, nu_ref, xs_ref, wg_ref, wu_ref, wd_ref, y_o):
    i = pl.program_id(0)

    @pl.when(i < nu_ref[0])
    def _():
        xb = xs_ref[...].astype(BF16)
        gate = _dot(xb, wg_ref[...])
        h = gate * _sigmoid(gate) * _dot(xb, wu_ref[...])
        y_o[...] = _dot(h.astype(BF16), wd_ref[...])

    @pl.when(i >= nu_ref[0])
    def _():
        y_o[...] = jnp.zeros_like(y_o)


def _experts(block_e, n_used, xs, w_gate, w_up, w_down):
    n_blocks = xs.shape[0] // SLOT_BLOCK
    blk = pl.BlockSpec((SLOT_BLOCK, D_MODEL), lambda i, be, nu: (i, 0))
    grid_spec = pltpu.PrefetchScalarGridSpec(
        num_scalar_prefetch=2,
        grid=(n_blocks,),
        in_specs=[blk,
                  pl.BlockSpec((None, D_MODEL, D_EXPERT), lambda i, be, nu: (be[i], 0, 0)),
                  pl.BlockSpec((None, D_MODEL, D_EXPERT), lambda i, be, nu: (be[i], 0, 0)),
                  pl.BlockSpec((None, D_EXPERT, D_MODEL), lambda i, be, nu: (be[i], 0, 0))],
        out_specs=blk)
    return pl.pallas_call(
        _expert_kernel,
        grid_spec=grid_spec,
        out_shape=jax.ShapeDtypeStruct(xs.shape, F32),
        compiler_params=_params("arbitrary"),
        name="experts",
    )(block_e, n_used, xs, w_gate, w_up, w_down)


def _combine_kernel(dest_ref, x_ref, p_ref, wt_ref, wpg, wple, g_ref, b_ref, yb_ref, x2_o, buf, sem, *, tm):
    i = pl.program_id(0)

    def row_copy(r, k):
        return pltpu.make_async_copy(yb_ref.at[pl.ds(dest_ref[2 * (i * tm + r) + k], 1)],
                                     buf.at[k, pl.ds(r, 1)], sem.at[k])

    def issue(r, c):
        row_copy(r, 0).start()
        row_copy(r, 1).start()
        return c

    def drain(r, c):
        row_copy(r, 0).wait()
        row_copy(r, 1).wait()
        return c

    lax.fori_loop(0, tm, issue, 0)
    x = x_ref[...]
    ple = _sigmoid(_dot(x.astype(BF16), wpg[...])) * _dot(p_ref[...].astype(BF16), wple[...])
    lax.fori_loop(0, tm, drain, 0)
    wt = wt_ref[...]
    y = wt[:, 0:1] * buf[0] + wt[:, 1:2] * buf[1]
    x2_o[...] = _layer_norm(DEEPNORM_ALPHA * x + y + ple, g_ref[...], b_ref[...])


def _combine(dest, x1, p, wt, wpg, wple, g, b, yb, tm):
    n = x1.shape[0]
    row = lambda i, d: (i, 0)
    grid_spec = pltpu.PrefetchScalarGridSpec(
        num_scalar_prefetch=1,
        grid=(n // tm,),
        in_specs=[pl.BlockSpec((tm, D_MODEL), row), pl.BlockSpec((tm, PLE_DIM), row),
                  pl.BlockSpec((tm, SUBLANES), row)]
        + [pl.BlockSpec(a.shape, lambda i, d: (0, 0)) for a in (wpg, wple, g, b)]
        + [pl.BlockSpec(memory_space=pl.ANY)],
        out_specs=pl.BlockSpec((tm, D_MODEL), row),
        scratch_shapes=[pltpu.VMEM((2, tm, D_MODEL), F32), pltpu.SemaphoreType.DMA((2,))])
    return pl.pallas_call(
        functools.partial(_combine_kernel, tm=tm),
        grid_spec=grid_spec,
        out_shape=jax.ShapeDtypeStruct((n, D_MODEL), F32),
        compiler_params=_params("arbitrary"),
        name="combine",
    )(dest, x1, p, wt, wpg, wple, g, b, yb)


def _tile(n, pref):
    return pref if n % pref == 0 else n


def _rope_tables(pos):
    def tab(half, reps):
        freq = ROPE_THETA ** (-jnp.arange(half, dtype=F32) / half)
        ang = pos.astype(F32)[:, None] * freq[None, :]
        cos, sin = jnp.cos(ang), jnp.sin(ang)
        return jnp.tile(jnp.concatenate([cos, cos], -1), (1, reps)), jnp.tile(jnp.concatenate([-sin, sin], -1), (1, reps))
    return tab(HEAD_DIM // 2, 1) + tab(IDX_DIM // 2, 2)


def _split_w_in(w_in):
    sizes = (N_HEADS * HEAD_DIM, N_KV_HEADS * HEAD_DIM, N_KV_HEADS * HEAD_DIM, N_IDX_HEADS * IDX_DIM,
             N_IDX_HEADS, IDX_DIM, D_MODEL, D_MODEL, D_MODEL, 3 * D_MODEL)
    parts, o = [], 0
    for s in sizes:
        parts.append(w_in[:, o:o + s].astype(BF16))
        o += s
    wq, wk, wv, wiq, wiw, wik, wxr, wgr, wxq, wgz = parts
    wiw = jnp.pad(wiw, ((0, 0), (0, LANES - N_IDX_HEADS)))
    wik = jnp.concatenate([wik, wik], axis=1)
    return (wq, wk, wv, wiq, wiw, wik), (wxr, wgr, wxq), wgz


def _ffn(x1, p, lw):
    n = x1.shape[0]
    tm = _tile(n, 512)
    ei, wt, counts = _router(x1, lw["w_rt"], lw["b_rt"], tm)
    counts = counts[:, 0].astype(I32)
    padded = (counts + SLOT_BLOCK - 1) // SLOT_BLOCK * SLOT_BLOCK
    pad_end = jnp.cumsum(padded)
    pad_start = pad_end - padded
    dest = (pad_start[ei[0:2]] + ei[2:4]).T.reshape(-1)
    n_blocks = -(-2 * n // SLOT_BLOCK) + N_EXPERTS
    block_e = jnp.minimum(jnp.searchsorted(pad_end, jnp.arange(n_blocks, dtype=I32) * SLOT_BLOCK, side="right"),
                          N_EXPERTS - 1).astype(I32)
    n_used = (pad_end[-1:] // SLOT_BLOCK).astype(I32)
    xs = _dispatch(dest, x1, n_blocks * SLOT_BLOCK, tm)
    yb = _experts(block_e, n_used, xs, lw["w_gate"], lw["w_up"], lw["w_down"])
    tc = _tile(n, 256)
    return _combine(dest, x1, p, wt.T, lw["w_ple_gate"], lw["w_ple"], lw["ln2_g"], lw["ln2_b"], yb, tc)


def kernel(x_prompt, x_sample, p_prompt, p_sample, cache_k, cache_v, cache_idx, state_lru_h, state_lru_conv, state_pool, page_table, w_in, w_out, lru_conv_w, lru_conv_b, lru_wa, lru_ba, lru_wi, lru_bi, lru_lambda, pool_w, pool_scale, ln1_g, ln1_b, w_router_group, b_router_group, w_router_expert, b_router_expert, w_exp_gate, w_exp_up, w_exp_down, w_ple, w_ple_gate, ln2_g, ln2_b):
    bp, tp = x_prompt.shape[:2]
    bs, ts = x_sample.shape[:2]
    assert ts == 1, "the sample group decodes one token per sequence"
    depth = w_in.shape[0]
    past = page_table.shape[1] * PAGE_SIZE
    n_p = bp * tp
    tm_p = _tile(tp, 512)
    tq = _tile(tp, 128)
    tt = _tile(tp, 256)
    tabs_p = _rope_tables(jnp.arange(tp))
    tabs_s = _rope_tables(jnp.full((bs,), past, I32))

    xp = x_prompt.reshape(n_p, D_MODEL)
    xs = x_sample.reshape(bs, D_MODEL)
    outs = [[] for _ in range(12)]
    row2 = lambda a: a.reshape(1, -1)
    for i in range(depth):
        w_attn, w_mix, wgz = _split_w_in(w_in[i])
        mix_w = w_mix + (lru_conv_w[i], row2(lru_conv_b[i]), lru_wa[i].astype(BF16), row2(lru_ba[i]),
                         lru_wi[i].astype(BF16), row2(lru_bi[i]), row2(lru_lambda[i]),
                         pool_w[i].astype(BF16), row2(pool_scale[i]))
        wout = w_out[i].astype(BF16)
        g1, b1 = row2(ln1_g[i]), row2(ln1_b[i])
        w_rt = jnp.concatenate([w_router_expert[i].T, w_router_group[i].T,
                                jnp.zeros((ROUTER_ROWS - N_EXPERTS - N_EXPERT_GROUPS, D_MODEL), F32)], 0)
        b_rt = jnp.concatenate([b_router_expert[i], b_router_group[i],
                                jnp.zeros((ROUTER_ROWS - N_EXPERTS - N_EXPERT_GROUPS,), F32)])[:, None]
        lw = dict(w_rt=w_rt, b_rt=b_rt, w_gate=w_exp_gate[i].astype(BF16), w_up=w_exp_up[i].astype(BF16),
                  w_down=w_exp_down[i].astype(BF16), w_ple_gate=w_ple_gate[i].astype(BF16),
                  w_ple=w_ple[i].astype(BF16), ln2_g=row2(ln2_g[i]), ln2_b=row2(ln2_b[i]))

        q, k, v, kb, vb, iq, iw, ik = _proj_attn(xp, w_attn, tabs_p, tm_p, tp // tm_p)
        y_att = _dsa_prompt(q, kb, vb, iq, iw, ik, bp, tp, tq)
        y_lru, y_pool, h_new, conv_new, pool_new = _mix_prompt(xp, mix_w, bp, tp, tt)
        x1 = _merge(xp, y_att, y_lru, y_pool, wgz, wout, g1, b1, _tile(n_p, 256))
        xp = _ffn(x1, p_prompt[i].reshape(n_p, PLE_DIM), lw)
        outs[0].append(k.reshape(bp, tp, N_KV_HEADS, HEAD_DIM))
        outs[1].append(v.reshape(bp, tp, N_KV_HEADS, HEAD_DIM))
        outs[2].append(ik[:, :IDX_DIM].reshape(bp, tp, IDX_DIM))
        outs[3].append(h_new[:, 0])
        outs[4].append(conv_new[:, SUBLANES - (CONV_WIDTH - 1):])
        outs[5].append(pool_new[:, 2 * SUBLANES - POOL_BUF:])

        q, k, v, _, _, iq, iw, ik = _proj_attn(xs, w_attn, tabs_s, bs, 1)
        sel_idx = _dsa_sample_select(page_table, iq, iw, ik[:, :IDX_DIM], cache_idx, i)
        y_att = _dsa_sample_attend(sel_idx[:, :, 0], page_table, q, k, v, cache_k, cache_v, i)
        y_lru, y_pool, h_new, xr, xq = _mix_sample(xs, mix_w, state_lru_conv[i], state_lru_h[i], state_pool[i], past)
        x1 = _merge(xs, y_att, y_lru, y_pool, wgz, wout, g1, b1, bs)
        xs = _ffn(x1, p_sample[i].reshape(bs, PLE_DIM), lw)
        outs[6].append(k.reshape(bs, ts, N_KV_HEADS, HEAD_DIM))
        outs[7].append(v.reshape(bs, ts, N_KV_HEADS, HEAD_DIM))
        outs[8].append(ik[:, :IDX_DIM].reshape(bs, ts, IDX_DIM))
        outs[9].append(h_new)
        outs[10].append(jnp.concatenate([state_lru_conv[i][:, 1:], xr[:, None]], 1))
        outs[11].append(jnp.concatenate([state_pool[i][:, 1:], xq[:, None]], 1))

    return (xp.reshape(bp, tp, D_MODEL), xs.reshape(bs, ts, D_MODEL)) + tuple(jnp.stack(o) for o in outs)
```

```python
import functools

import jax
import jax.numpy as jnp
import numpy as np
from jax import lax
from jax.experimental import pallas as pl
from jax.experimental.pallas import tpu as pltpu

F32 = jnp.float32
BF16 = jnp.bfloat16
I32 = jnp.int32

D_MODEL = 1024
N_HEADS = 8
HEAD_DIM = 128
N_KV_HEADS = 4
KV_GROUP = N_HEADS // N_KV_HEADS
N_IDX_HEADS = 8
IDX_DIM = 64
INDEX_SCALE = (IDX_DIM * N_IDX_HEADS) ** -0.5
TOPK_MAX = 256
ROPE_THETA = 10000.0
PAGE_SIZE = 128
N_LRU_BLOCKS = 8
LRU_BLOCK = D_MODEL // N_LRU_BLOCKS
CONV_WIDTH = 4
LRU_C = 8.0
POOL_WINDOWS = (2, 4, 8, 16)
POOL_GROUP = D_MODEL // len(POOL_WINDOWS)
POOL_BUF = max(POOL_WINDOWS) - 1
N_EXPERT_GROUPS = 4
EXPERTS_PER_GROUP = 8
N_EXPERTS = N_EXPERT_GROUPS * EXPERTS_PER_GROUP
D_EXPERT = 512
PLE_DIM = 256
LN_EPS = 1e-5
DEPTH = 2
DEEPNORM_ALPHA = (2 * DEPTH) ** 0.25
ATTN_SCALE = HEAD_DIM ** -0.5

LANES = 128
SUBLANES = 8
SLOT_BLOCK = 256
DSA_EXTENTS = 4
SEQ_GROUP = 8
DMA_UNROLL = 8
VMEM_LIMIT = 48 * 1024 * 1024
INT_MIN = -2 ** 31
KEY_NEG_INF = INT_MIN + 0x7FFFFF

_NT = (((1,), (1,)), ((), ()))


def _dot(a, b):
    return jnp.dot(a, b, preferred_element_type=F32)


def _dot_nt(a, b):
    return lax.dot_general(a, b, _NT, preferred_element_type=F32)


def _sigmoid(x):
    return 1.0 / (1.0 + jnp.exp(-x))


def _gelu_tanh(x):
    c = np.float32(np.sqrt(2.0 / np.pi))
    return x * (0.5 * (1.0 + jnp.tanh(c * (x + 0.044715 * (x * x * x)))))


def _softplus(x):
    return jnp.maximum(x, 0.0) + jnp.log1p(jnp.exp(-jnp.abs(x)))


def _layer_norm(r, g, b):
    mu = jnp.mean(r, axis=-1, keepdims=True)
    c = r - mu
    var = jnp.mean(c * c, axis=-1, keepdims=True)
    return c * lax.rsqrt(var + LN_EPS) * g + b


def _sort_key(x):
    bits = pltpu.bitcast(jnp.where(x == 0.0, 0.0, x), I32)
    return jnp.where(bits >= 0, bits, bits ^ 0x7FFFFFFF)


def _params(*sem):
    return pltpu.CompilerParams(dimension_semantics=sem, vmem_limit_bytes=VMEM_LIMIT)


def _full(shape):
    n = len(shape)
    return pl.BlockSpec(shape, lambda *_: (0,) * n)


def _proj_attn_kernel(x_ref, wq, wk, wv, wiq, wiw, wik, cos_ref, sin_ref, cosi_ref, sini_ref,
                      q_o, k_o, v_o, kb_o, vb_o, iq_o, iw_o, ik_o):
    xb = x_ref[...].astype(BF16)
    cos, sin = cos_ref[...], sin_ref[...]
    cosi, sini = cosi_ref[...], sini_ref[...]
    lane = lax.broadcasted_iota(I32, cos.shape, 1)
    low_half = (lane & (IDX_DIM - 1)) < (IDX_DIM // 2)

    def rope128(z):
        return z * cos + pltpu.roll(z, HEAD_DIM // 2, 1) * sin

    def rope64(z):
        partner = jnp.where(low_half, pltpu.roll(z, LANES - IDX_DIM // 2, 1), pltpu.roll(z, IDX_DIM // 2, 1))
        return z * cosi + partner * sini

    q = _dot(xb, wq[...])
    for h in range(N_HEADS):
        sl = slice(h * LANES, (h + 1) * LANES)
        q_o[:, sl] = (rope128(q[:, sl]) * ATTN_SCALE).astype(BF16)
    k = _dot(xb, wk[...])
    for c in range(N_KV_HEADS):
        sl = slice(c * LANES, (c + 1) * LANES)
        kr = rope128(k[:, sl])
        k_o[:, sl] = kr
        kb_o[:, sl] = kr.astype(BF16)
    v = _dot(xb, wv[...])
    v_o[...] = v
    vb_o[...] = v.astype(BF16)
    iq = _dot(xb, wiq[...])
    for j in range(N_IDX_HEADS * IDX_DIM // LANES):
        sl = slice(j * LANES, (j + 1) * LANES)
        iq_o[:, sl] = rope64(iq[:, sl])
    iw_o[...] = _dot(xb, wiw[...])
    ik_o[...] = rope64(_dot(xb, wik[...]))


def _proj_attn(x, w, tabs, tm, n_tab_blocks):
    n = x.shape[0]
    cos, sin, cosi, sini = tabs
    row = lambda i: (i, 0)
    tab = lambda i: (i % n_tab_blocks, 0)
    wspec = lambda a: _full(a.shape)
    outs = [
        ((n, 1024), BF16), ((n, 512), F32), ((n, 512), F32), ((n, 512), BF16), ((n, 512), BF16),
        ((n, 512), F32), ((n, LANES), F32), ((n, LANES), F32)]
    return pl.pallas_call(
        _proj_attn_kernel,
        grid=(n // tm,),
        in_specs=[pl.BlockSpec((tm, D_MODEL), row)] + [wspec(a) for a in w]
        + [pl.BlockSpec((tm, LANES), tab)] * 4,
        out_specs=[pl.BlockSpec((tm, s[1]), row) for s, _ in outs],
        out_shape=[jax.ShapeDtypeStruct(s, d) for s, d in outs],
        compiler_params=_params("parallel"),
        name="proj_attn",
    )(x, *w, cos, sin, cosi, sini)


def _count_rows(mask):
    return jnp.sum(jnp.where(mask, 1.0, 0.0), axis=1, keepdims=True)


def _dsa_prompt_kernel(q_ref, kb_ref, vb_ref, iq_ref, iw_ref, ik_ref, *rest, tq, t_len, q0, topk, idx_bits):
    o_ref, p_sc = rest[-2:]
    qi = q0 + pl.program_id(1)
    ikb = ik_ref[...].astype(BF16)
    iq = iq_ref[...]
    iw = iw_ref[...] * INDEX_SCALE
    lane = lax.broadcasted_iota(I32, (tq, LANES), 1)
    score = jnp.zeros((tq, t_len), F32)
    for h in range(N_IDX_HEADS):
        chunk = iq[:, (h // 2) * LANES:(h // 2 + 1) * LANES]
        keep = (lane < IDX_DIM) if h % 2 == 0 else (lane >= IDX_DIM)
        s = _dot_nt(jnp.where(keep, chunk, 0.0).astype(BF16), ikb)
        score = score + jnp.maximum(s, 0.0) * iw[:, h:h + 1]
    qpos = qi * tq + lax.broadcasted_iota(I32, (tq, 1), 0)
    kpos = lax.broadcasted_iota(I32, (tq, t_len), 1)
    causal = kpos <= qpos
    key = _sort_key(jnp.where(causal, score, -jnp.inf))

    def thr_body(b, t):
        cand = t + lax.shift_left(jnp.int32(1), 31 - b)
        return jnp.where(_count_rows(key >= cand) >= topk, cand, t)

    thr = lax.fori_loop(0, 32, thr_body, jnp.full((tq, 1), INT_MIN, I32))
    gt = key > thr
    eq = key == thr
    need = topk - _count_rows(gt)
    tie = (_count_rows(eq) > need) & (thr > KEY_NEG_INF)
    p_sc[...] = jnp.full((tq, 1), t_len, I32)

    @pl.when(jnp.max(jnp.where(tie, 1.0, 0.0)) > 0.0)
    def _():
        def pos_body(b, p):
            cand = p + lax.shift_left(jnp.int32(1), idx_bits - 1 - b)
            return jnp.where(_count_rows(eq & (kpos < cand)) < need, cand, p)
        p_sc[...] = lax.fori_loop(0, idx_bits, pos_body, jnp.zeros((tq, 1), I32))

    sel = causal & (gt | (eq & (kpos <= p_sc[...])))
    bias = jnp.where(sel, 0.0, -jnp.inf)
    for c in range(N_KV_HEADS):
        kc = kb_ref[:, c * LANES:(c + 1) * LANES]
        vc = vb_ref[:, c * LANES:(c + 1) * LANES]
        for g in range(KV_GROUP):
            h = c * KV_GROUP + g
            s = _dot_nt(q_ref[:, h * LANES:(h + 1) * LANES], kc) + bias
            m = jnp.max(s, axis=1, keepdims=True)
            p = jnp.exp(s - m)
            l = jnp.sum(p, axis=1, keepdims=True)
            o_ref[:, h * LANES:(h + 1) * LANES] = _dot(p.astype(BF16), vc) / l


def _dsa_prompt(q, kb, vb, iq, iw, ik, batch, t_len, tq):
    topk = min(TOPK_MAX, t_len // 4)
    nq = t_len // tq
    n_var = DSA_EXTENTS if nq % DSA_EXTENTS == 0 else 1
    per = nq // n_var
    kb, vb, ik = (a.reshape(batch, t_len, a.shape[-1]) for a in (kb, vb, ik))
    y = jnp.zeros((batch * t_len, 1024), F32)
    for v in range(n_var):
        ext = (v + 1) * per * tq
        qrow = lambda b, i, v=v: (b * nq + v * per + i, 0)
        brow = lambda b, i: (b, 0, 0)
        in_specs = [pl.BlockSpec((tq, 1024), qrow), pl.BlockSpec((None, ext, 512), brow),
                    pl.BlockSpec((None, ext, 512), brow), pl.BlockSpec((tq, 512), qrow),
                    pl.BlockSpec((tq, LANES), qrow), pl.BlockSpec((None, ext, LANES), brow)]
        in_specs.append(pl.BlockSpec(memory_space=pl.ANY))
        y = pl.pallas_call(
            functools.partial(_dsa_prompt_kernel, tq=tq, t_len=ext, q0=v * per, topk=topk,
                              idx_bits=max(1, int(np.ceil(np.log2(ext))))),
            grid=(batch, per),
            in_specs=in_specs,
            out_specs=pl.BlockSpec((tq, 1024), qrow),
            out_shape=jax.ShapeDtypeStruct((batch * t_len, 1024), F32),
            scratch_shapes=[pltpu.VMEM((tq, 1), I32)],
            input_output_aliases={6: 0},
            compiler_params=_params("parallel", "arbitrary"),
            name="dsa_prompt",
        )(q, kb, vb, iq, iw, ik, y)
    return y


def _dsa_sample_select_kernel(pt_ref, iq_ref, iw_ref, ikn_ref, cidx_ref, idx_o,
                              ikbuf, s_sc, c_sc, sem, *, layer, n_pages, topk, idx_bits, group):
    g0 = pl.program_id(0) * group
    past = n_pages * PAGE_SIZE

    def page_copy(s, j):
        slot = s % 2
        return pltpu.make_async_copy(cidx_ref.at[layer, pt_ref[(g0 + s) * n_pages + j]],
                                     ikbuf.at[slot, j], sem.at[slot])

    def issue_seq(s):
        def body(j, c):
            page_copy(s, j).start()
            return c
        lax.fori_loop(0, n_pages, body, 0, unroll=DMA_UNROLL)

    def drain_seq(s):
        def body(j, c):
            page_copy(s, j).wait()
            return c
        lax.fori_loop(0, n_pages, body, 0, unroll=DMA_UNROLL)

    issue_seq(0)

    def seq_body(s, c):
        @pl.when(s + 1 < group)
        def _():
            issue_seq(s + 1)
        drain_seq(s)
        slot = s % 2
        iq8 = iq_ref[s].astype(BF16)
        w8 = iw_ref[s] * INDEX_SCALE
        for j in range(n_pages):
            s8 = _dot(iq8, ikbuf[slot, j].astype(BF16))
            s_sc[s, j:j + 1, :] = jnp.sum(jnp.maximum(s8, 0.0) * w8, axis=0, keepdims=True)
        return c

    lax.fori_loop(0, group, seq_body, 0)

    w_all = iw_ref[...] * INDEX_SCALE
    own = jnp.sum(iq_ref[...].astype(BF16).astype(F32) * ikn_ref[...].astype(BF16).astype(F32),
                  axis=2, keepdims=True)
    own = jnp.sum(jnp.maximum(own, 0.0) * w_all, axis=1, keepdims=True)

    key = _sort_key(s_sc[...])
    key_own = _sort_key(own)
    pos = (lax.broadcasted_iota(I32, key.shape, 1) * PAGE_SIZE + lax.broadcasted_iota(I32, key.shape, 2))

    def count(mask, mask_own):
        c = jnp.sum(jnp.where(mask, 1.0, 0.0), axis=1, keepdims=True)
        return jnp.sum(c, axis=2, keepdims=True) + jnp.where(mask_own, 1.0, 0.0)

    def thr_body(i, t):
        cand = t + lax.shift_left(jnp.int32(1), 31 - i)
        return jnp.where(count(key >= cand, key_own >= cand) >= topk, cand, t)

    thr = lax.fori_loop(0, 32, thr_body, jnp.full((group, 1, 1), INT_MIN, I32))
    gt, eq = key > thr, key == thr
    gt_own, eq_own = key_own > thr, key_own == thr
    need = topk - count(gt, gt_own)

    def pos_body(i, p):
        cand = p + lax.shift_left(jnp.int32(1), idx_bits - 1 - i)
        return jnp.where(count(eq & (pos < cand), eq_own & (past < cand)) < need, cand, p)

    plast = lax.fori_loop(0, idx_bits, pos_body, jnp.zeros((group, 1, 1), I32))
    s_sc[...] = jnp.where(gt | (eq & (pos <= plast)), 1.0, 0.0)

    r_i = lax.broadcasted_iota(I32, (PAGE_SIZE, PAGE_SIZE), 0)
    c_i = lax.broadcasted_iota(I32, (PAGE_SIZE, PAGE_SIZE), 1)
    tri_incl = jnp.where(r_i <= c_i, 1.0, 0.0).astype(BF16)
    pr = lax.broadcasted_iota(I32, (n_pages, n_pages), 0)
    pc = lax.broadcasted_iota(I32, (n_pages, n_pages), 1)
    tri_pages = jnp.where(pc < pr, 1.0, 0.0).astype(BF16)
    rank = lax.broadcasted_iota(I32, (topk, PAGE_SIZE), 0).astype(F32)

    def compact(s, c):
        within = _dot(s_sc[s].astype(BF16), tri_incl)
        tot = jnp.broadcast_to(within[:, PAGE_SIZE - 1:PAGE_SIZE], (n_pages, PAGE_SIZE)).astype(BF16)
        c_sc[...] = within + _dot(tri_pages, tot)

        def acc_body(j, acc):
            return acc + jnp.where(c_sc[pl.ds(j, 1), :] <= rank, 1.0, 0.0)

        acc = lax.fori_loop(0, n_pages, acc_body, jnp.zeros((topk, PAGE_SIZE), F32))
        idx_o[s] = jnp.sum(acc, axis=1, keepdims=True).astype(I32)
        return c

    lax.fori_loop(0, group, compact, 0)


def _dsa_sample_select(page_table, iq, iw, ik_new, cache_idx_t, layer):
    bd, n_pages = page_table.shape
    past = n_pages * PAGE_SIZE
    topk = min(TOPK_MAX, (past + 1) // 4)
    idx_bits = int(np.floor(np.log2(past))) + 1
    group = SEQ_GROUP if bd % SEQ_GROUP == 0 else bd
    grp = lambda g, pt: (g, 0, 0)
    grid_spec = pltpu.PrefetchScalarGridSpec(
        num_scalar_prefetch=1,
        grid=(bd // group,),
        in_specs=[pl.BlockSpec((group, N_IDX_HEADS, IDX_DIM), grp),
                  pl.BlockSpec((group, N_IDX_HEADS, 1), grp),
                  pl.BlockSpec((group, 1, IDX_DIM), grp),
                  pl.BlockSpec(memory_space=pl.ANY)],
        out_specs=pl.BlockSpec((group, topk, 1), grp),
        scratch_shapes=[pltpu.VMEM((2, n_pages, IDX_DIM, PAGE_SIZE), F32), pltpu.VMEM((group, n_pages, PAGE_SIZE), F32),
                        pltpu.VMEM((n_pages, PAGE_SIZE), F32), pltpu.SemaphoreType.DMA((2,))])
    return pl.pallas_call(
        functools.partial(_dsa_sample_select_kernel, layer=layer, n_pages=n_pages, topk=topk, idx_bits=idx_bits,
                          group=group),
        grid_spec=grid_spec,
        out_shape=jax.ShapeDtypeStruct((bd, topk, 1), I32),
        compiler_params=_params("arbitrary"),
        name="dsa_sample_select",
    )(page_table.reshape(-1), iq.reshape(bd, N_IDX_HEADS, IDX_DIM), iw[:, :N_IDX_HEADS, None],
      ik_new[:, None, :], cache_idx_t)


def _dsa_sample_attend_kernel(idx_ref, pt_ref, q_ref, kn_ref, vn_ref, idxv_ref, ck_ref, cv_ref, o_ref,
                              kbuf, vbuf, sem, *, layer, n_pages, topk):
    b = pl.program_id(0)
    past = n_pages * PAGE_SIZE

    def row_copies(r):
        pidx = jnp.minimum(idx_ref[b * topk + r], past - 1)
        phys = pt_ref[b * n_pages + pidx // PAGE_SIZE]
        off = pidx % PAGE_SIZE
        return (pltpu.make_async_copy(ck_ref.at[layer, phys, off], kbuf.at[r], sem.at[0]),
                pltpu.make_async_copy(cv_ref.at[layer, phys, off], vbuf.at[r], sem.at[1]))

    def issue(r, c):
        for cp in row_copies(r):
            cp.start()
        return c

    def drain(r, c):
        for cp in row_copies(r):
            cp.wait()
        return c

    lax.fori_loop(0, topk, issue, 0, unroll=DMA_UNROLL)
    lax.fori_loop(0, topk, drain, 0, unroll=DMA_UNROLL)

    own = idxv_ref[...] >= past
    k_sel = jnp.where(own, kn_ref[...][None], kbuf[...])
    v_sel = jnp.where(own, vn_ref[...][None], vbuf[...])
    for g in range(KV_GROUP):
        s = jnp.sum(k_sel * q_ref[g][None], axis=-1, keepdims=True)
        m = jnp.max(s, axis=0, keepdims=True)
        p = jnp.exp(s - m)
        l = jnp.sum(p, axis=0)
        o_ref[g] = jnp.sum(p * v_sel, axis=0) / l


def _dsa_sample_attend(idx, page_table, q, k_new, v_new, cache_k, cache_v, layer):
    bd, n_pages = page_table.shape
    topk = idx.shape[1]
    qg = q.astype(F32).reshape(bd, N_KV_HEADS, KV_GROUP, HEAD_DIM).transpose(0, 2, 1, 3)
    head = lambda b, *_: (b, 0, 0)
    grid_spec = pltpu.PrefetchScalarGridSpec(
        num_scalar_prefetch=2,
        grid=(bd,),
        in_specs=[pl.BlockSpec((None, KV_GROUP, N_KV_HEADS, HEAD_DIM), lambda b, *_: (b, 0, 0, 0)),
                  pl.BlockSpec((None, N_KV_HEADS, HEAD_DIM), head),
                  pl.BlockSpec((None, N_KV_HEADS, HEAD_DIM), head),
                  pl.BlockSpec((None, topk, 1, 1), lambda b, *_: (b, 0, 0, 0)),
                  pl.BlockSpec(memory_space=pl.ANY), pl.BlockSpec(memory_space=pl.ANY)],
        out_specs=pl.BlockSpec((None, KV_GROUP, N_KV_HEADS, HEAD_DIM), lambda b, *_: (b, 0, 0, 0)),
        scratch_shapes=[pltpu.VMEM((topk, N_KV_HEADS, HEAD_DIM), F32), pltpu.VMEM((topk, N_KV_HEADS, HEAD_DIM), F32),
                        pltpu.SemaphoreType.DMA((2,))])
    o = pl.pallas_call(
        functools.partial(_dsa_sample_attend_kernel, layer=layer, n_pages=n_pages, topk=topk),
        grid_spec=grid_spec,
        out_shape=jax.ShapeDtypeStruct((bd, KV_GROUP, N_KV_HEADS, HEAD_DIM), F32),
        compiler_params=_params("arbitrary"),
        name="dsa_sample_attend",
    )(idx.reshape(-1), page_table.reshape(-1), qg, k_new.reshape(bd, N_KV_HEADS, HEAD_DIM),
      v_new.reshape(bd, N_KV_HEADS, HEAD_DIM), idx.reshape(bd, topk, 1, 1), cache_k, cache_v)
    return o.transpose(0, 2, 1, 3).reshape(bd, N_HEADS * HEAD_DIM)


def _lru_gates(xc, wa, ba, wi, bi, lam):
    xcb = xc.astype(BF16)
    a_parts, u_parts = [], []
    for n in range(N_LRU_BLOCKS):
        sl = slice(n * LRU_BLOCK, (n + 1) * LRU_BLOCK)
        r = _sigmoid(_dot(xcb[:, sl], wa[n]) + ba[:, sl])
        i = _sigmoid(_dot(xcb[:, sl], wi[n]) + bi[:, sl])
        log_a = -LRU_C * r * _softplus(-lam[:, sl])
        a_parts.append(jnp.exp(log_a))
        th = jnp.tanh(log_a)
        u_parts.append(jnp.sqrt(-2.0 * th / (1.0 - th)) * i * xc[:, sl])
    return a_parts, u_parts


def _pool_mix(window_sum, xq, cnt, pool_w, pool_scale, g):
    sl = slice(g * POOL_GROUP, (g + 1) * POOL_GROUP)
    pooled = window_sum / cnt - xq[:, sl]
    return _dot(pooled.astype(BF16), pool_w[g]) * pool_scale[:, sl]


def _mix_prompt_kernel(x_ref, wxr, wgr, wxq, convw, convb, wa, ba, wi, bi, lam, poolw, pscale,
                       ylru_o, ypool_o, h_o, conv_o, pool_o,
                       xr_ext, xq_ext, a0, u0, a1, u1, h_sc, *, tt, pad):
    t = pl.program_id(1)
    halo_r, halo_q = SUBLANES, 2 * SUBLANES

    @pl.when(t == 0)
    def _():
        xr_ext[0:halo_r] = jnp.zeros((halo_r, D_MODEL), F32)
        xq_ext[0:halo_q] = jnp.zeros((halo_q, D_MODEL), F32)
        h_sc[...] = jnp.zeros_like(h_sc)
        a0[0:pad] = jnp.ones((pad, D_MODEL), F32)
        a1[0:pad] = jnp.ones((pad, D_MODEL), F32)
        u0[0:pad] = jnp.zeros((pad, D_MODEL), F32)
        u1[0:pad] = jnp.zeros((pad, D_MODEL), F32)

    @pl.when(t > 0)
    def _():
        xr_ext[0:halo_r] = xr_ext[tt:tt + halo_r]
        xq_ext[0:halo_q] = xq_ext[tt:tt + halo_q]

    xb = x_ref[...].astype(BF16)
    xr = _dot(xb, wxr[...])
    xq = _dot(xb, wxq[...])
    xr_ext[halo_r:halo_r + tt] = xr
    xq_ext[halo_q:halo_q + tt] = xq

    cw = convw[...]
    xc = convb[...] + cw[CONV_WIDTH - 1:CONV_WIDTH] * xr
    for j in range(CONV_WIDTH - 1):
        o = halo_r - (CONV_WIDTH - 1) + j
        xc = xc + cw[j:j + 1] * xr_ext[o:o + tt]
    a_parts, u_parts = _lru_gates(xc, wa, ba[...], wi, bi[...], lam[...])
    for n in range(N_LRU_BLOCKS):
        sl = slice(n * LRU_BLOCK, (n + 1) * LRU_BLOCK)
        a0[pad:pad + tt, sl] = a_parts[n]
        u0[pad:pad + tt, sl] = u_parts[n]

    bufs = ((a0, u0), (a1, u1))
    d, cur = 1, 0
    while d < tt:
        (sa, su), (da, du) = bufs[cur], bufs[1 - cur]
        a_cur, u_cur = sa[pad:pad + tt], su[pad:pad + tt]
        du[pad:pad + tt] = u_cur + a_cur * su[pad - d:pad - d + tt]
        da[pad:pad + tt] = a_cur * sa[pad - d:pad - d + tt]
        d, cur = 2 * d, 1 - cur
    sa, su = bufs[cur]
    h = sa[pad:pad + tt] * h_sc[...] + su[pad:pad + tt]
    h_sc[...] = h[tt - 1:tt]
    ylru_o[...] = h * _gelu_tanh(_dot(xb, wgr[...]))

    posn = t * tt + lax.broadcasted_iota(I32, (tt, 1), 0)
    for g, w in enumerate(POOL_WINDOWS):
        sl = slice(g * POOL_GROUP, (g + 1) * POOL_GROUP)
        acc = xq[:, sl]
        for j in range(1, w):
            acc = acc + xq_ext[halo_q - j:halo_q - j + tt, sl]
        cnt = jnp.minimum(posn + 1, w).astype(F32)
        ypool_o[:, sl] = _pool_mix(acc, xq, cnt, poolw, pscale[...], g)

    @pl.when(t == pl.num_programs(1) - 1)
    def _():
        h_o[...] = h[tt - 1:tt]
        conv_o[...] = xr[tt - halo_r:tt]
        pool_o[...] = xq[tt - halo_q:tt]


def _mix_prompt(x, w, batch, t_len, tt):
    nt = t_len // tt
    pad = tt // 2
    row = lambda b, t: (b * nt + t, 0)
    per_b = lambda b, t: (b, 0, 0)
    outs = [((batch * t_len, D_MODEL), F32), ((batch * t_len, D_MODEL), F32),
            ((batch, 1, D_MODEL), F32), ((batch, SUBLANES, D_MODEL), F32), ((batch, 2 * SUBLANES, D_MODEL), F32)]
    return pl.pallas_call(
        functools.partial(_mix_prompt_kernel, tt=tt, pad=pad),
        grid=(batch, nt),
        in_specs=[pl.BlockSpec((tt, D_MODEL), row)] + [_full(a.shape) for a in w],
        out_specs=[pl.BlockSpec((tt, D_MODEL), row), pl.BlockSpec((tt, D_MODEL), row),
                   pl.BlockSpec((None, 1, D_MODEL), per_b), pl.BlockSpec((None, SUBLANES, D_MODEL), per_b),
                   pl.BlockSpec((None, 2 * SUBLANES, D_MODEL), per_b)],
        out_shape=[jax.ShapeDtypeStruct(s, d) for s, d in outs],
        scratch_shapes=[pltpu.VMEM((tt + SUBLANES, D_MODEL), F32), pltpu.VMEM((tt + 2 * SUBLANES, D_MODEL), F32)]
        + [pltpu.VMEM((pad + tt, D_MODEL), F32)] * 4 + [pltpu.VMEM((1, D_MODEL), F32)],
        compiler_params=_params("parallel", "arbitrary"),
        name="mix_prompt",
    )(x, *w)


def _mix_sample_kernel(x_ref, wxr, wgr, wxq, convw, convb, wa, ba, wi, bi, lam, poolw, pscale,
                       conv_ref, h_ref, pool_ref, ylru_o, ypool_o, h_o, xr_o, xq_o, *, cnt_pos):
    xb = x_ref[...].astype(BF16)
    xr = _dot(xb, wxr[...])
    xq = _dot(xb, wxq[...])
    cw = convw[...]
    xc = convb[...] + cw[CONV_WIDTH - 1:CONV_WIDTH] * xr
    for j in range(CONV_WIDTH - 1):
        xc = xc + cw[j:j + 1] * conv_ref[j]
    a_parts, u_parts = _lru_gates(xc, wa, ba[...], wi, bi[...], lam[...])
    gate = _gelu_tanh(_dot(xb, wgr[...]))
    for n in range(N_LRU_BLOCKS):
        sl = slice(n * LRU_BLOCK, (n + 1) * LRU_BLOCK)
        h = a_parts[n] * h_ref[:, sl] + u_parts[n]
        h_o[:, sl] = h
        ylru_o[:, sl] = h * gate[:, sl]
    for g, w in enumerate(POOL_WINDOWS):
        sl = slice(g * POOL_GROUP, (g + 1) * POOL_GROUP)
        acc = xq[:, sl]
        for j in range(1, w):
            acc = acc + pool_ref[POOL_BUF - j, :, sl]
        ypool_o[:, sl] = _pool_mix(acc, xq, float(min(cnt_pos, w)), poolw, pscale[...], g)
    xr_o[...] = xr
    xq_o[...] = xq


def _mix_sample(x, w, conv_state, h_state, pool_state, past):
    bd = x.shape[0]
    args = (x, *w, conv_state.transpose(1, 0, 2), h_state, pool_state.transpose(1, 0, 2))
    shp = jax.ShapeDtypeStruct((bd, D_MODEL), F32)
    return pl.pallas_call(
        functools.partial(_mix_sample_kernel, cnt_pos=past + 1),
        in_specs=[_full(a.shape) for a in args],
        out_specs=[_full((bd, D_MODEL))] * 5,
        out_shape=[shp] * 5,
        grid=(1,),
        compiler_params=_params("arbitrary"),
        name="mix_sample",
    )(*args)


def _merge_kernel(x_ref, ya_ref, yl_ref, yp_ref, wgz, wout, g_ref, b_ref, x1_o):
    x = x_ref[...]
    gz = _dot(x.astype(BF16), wgz[...])
    merged = (_sigmoid(gz[:, 0:D_MODEL]) * ya_ref[...] + _sigmoid(gz[:, D_MODEL:2 * D_MODEL]) * yl_ref[...]
              + _sigmoid(gz[:, 2 * D_MODEL:3 * D_MODEL]) * yp_ref[...])
    r = DEEPNORM_ALPHA * x + _dot(merged.astype(BF16), wout[...])
    x1_o[...] = _layer_norm(r, g_ref[...], b_ref[...])


def _merge(x, ya, yl, yp, wgz, wout, g, b, tm):
    n = x.shape[0]
    row = pl.BlockSpec((tm, D_MODEL), lambda i: (i, 0))
    return pl.pallas_call(
        _merge_kernel,
        grid=(n // tm,),
        in_specs=[row] * 4 + [_full(wgz.shape), _full(wout.shape), _full(g.shape), _full(b.shape)],
        out_specs=row,
        out_shape=jax.ShapeDtypeStruct((n, D_MODEL), F32),
        compiler_params=_params("parallel"),
        name="merge",
    )(x, ya, yl, yp, wgz, wout, g, b)


ROUTER_ROWS = 40


def _first_argmax(v, n):
    m = jnp.max(v, axis=0, keepdims=True)
    rows = lax.broadcasted_iota(I32, v.shape, 0)
    return m, jnp.min(jnp.where(v == m, rows, n), axis=0, keepdims=True)


def _router_kernel(x_ref, w_ref, b_ref, ei_o, wt_o, cnt_o, carry, *, tm):
    i = pl.program_id(0)

    @pl.when(i == 0)
    def _():
        carry[...] = jnp.zeros_like(carry)

    def split(v):
        hi = v.astype(BF16)
        return hi, (v - hi.astype(F32)).astype(BF16)

    x_hi, x_lo = split(x_ref[...])
    w_hi, w_lo = split(w_ref[...])
    logits = _dot_nt(w_hi, x_hi) + (_dot_nt(w_hi, x_lo) + _dot_nt(w_lo, x_hi)) + b_ref[...]
    le = logits[0:N_EXPERTS]
    lg = logits[N_EXPERTS:N_EXPERTS + N_EXPERT_GROUPS]

    gmax, gidx = _first_argmax(lg, N_EXPERT_GROUPS)
    p_top = 1.0 / jnp.sum(jnp.exp(lg - gmax), axis=0, keepdims=True)
    le_g = jnp.zeros((EXPERTS_PER_GROUP, tm), F32)
    for gi in range(N_EXPERT_GROUPS):
        le_g = le_g + jnp.where(gidx == gi, le[gi * EXPERTS_PER_GROUP:(gi + 1) * EXPERTS_PER_GROUP], 0.0)
    m1, i1 = _first_argmax(le_g, EXPERTS_PER_GROUP)
    rows8 = lax.broadcasted_iota(I32, le_g.shape, 0)
    m2, i2 = _first_argmax(jnp.where(rows8 == i1, -jnp.inf, le_g), EXPERTS_PER_GROUP)
    z = jnp.sum(jnp.exp(le_g - m1), axis=0, keepdims=True)
    p1 = 1.0 / z
    p2 = jnp.exp(m2 - m1) / z
    e1 = gidx * EXPERTS_PER_GROUP + i1
    e2 = gidx * EXPERTS_PER_GROUP + i2

    rows = lax.broadcasted_iota(I32, (N_EXPERTS, tm), 0)
    hit1, hit2 = rows == e1, rows == e2
    onehot = jnp.where(hit1 | hit2, 1.0, 0.0)
    r_i = lax.broadcasted_iota(I32, (tm, tm), 0)
    c_i = lax.broadcasted_iota(I32, (tm, tm), 1)
    before = _dot(onehot.astype(BF16), jnp.where(r_i < c_i, 1.0, 0.0).astype(BF16)) + carry[...]
    carry[...] = carry[...] + jnp.sum(onehot, axis=1, keepdims=True)

    ei_o[0:1, :] = e1
    ei_o[1:2, :] = e2
    ei_o[2:3, :] = jnp.sum(jnp.where(hit1, before, 0.0), axis=0, keepdims=True).astype(I32)
    ei_o[3:4, :] = jnp.sum(jnp.where(hit2, before, 0.0), axis=0, keepdims=True).astype(I32)
    ei_o[4:SUBLANES, :] = jnp.zeros((SUBLANES - 4, tm), I32)
    wt_o[0:1, :] = p1 / (p1 + p2) * p_top
    wt_o[1:2, :] = p2 / (p1 + p2) * p_top
    wt_o[2:SUBLANES, :] = jnp.zeros((SUBLANES - 2, tm), F32)
    cnt_o[...] = carry[...]


def _router(x1, w_rt, b_rt, tm):
    n = x1.shape[0]
    col = pl.BlockSpec((SUBLANES, tm), lambda i: (0, i))
    return pl.pallas_call(
        functools.partial(_router_kernel, tm=tm),
        grid=(n // tm,),
        in_specs=[pl.BlockSpec((tm, D_MODEL), lambda i: (i, 0)), _full(w_rt.shape), _full(b_rt.shape)],
        out_specs=[col, col, _full((N_EXPERTS, 1))],
        out_shape=[jax.ShapeDtypeStruct((SUBLANES, n), I32), jax.ShapeDtypeStruct((SUBLANES, n), F32),
                   jax.ShapeDtypeStruct((N_EXPERTS, 1), F32)],
        scratch_shapes=[pltpu.VMEM((N_EXPERTS, 1), F32)],
        compiler_params=_params("arbitrary"),
        name="router",
    )(x1, w_rt, b_rt)


def _dispatch_kernel(dest_ref, x_ref, xs_in, xs_out, sem, *, tm, n):
    del xs_in
    i = pl.program_id(0)

    def row_copy(r, k):
        return pltpu.make_async_copy(x_ref.at[pl.ds(r, 1)], xs_out.at[pl.ds(dest_ref[k * n + i * tm + r], 1)], sem)

    def issue(r, c):
        row_copy(r, 0).start()
        row_copy(r, 1).start()
        return c

    def drain(r, c):
        row_copy(r, 0).wait()
        row_copy(r, 1).wait()
        return c

    lax.fori_loop(0, tm, issue, 0, unroll=DMA_UNROLL)
    lax.fori_loop(0, tm, drain, 0, unroll=DMA_UNROLL)


def _dispatch(dest, x1, n_slots, tm):
    n = x1.shape[0]
    grid_spec = pltpu.PrefetchScalarGridSpec(
        num_scalar_prefetch=1,
        grid=(n // tm,),
        in_specs=[pl.BlockSpec((tm, D_MODEL), lambda i, d: (i, 0)), pl.BlockSpec(memory_space=pl.ANY)],
        out_specs=pl.BlockSpec(memory_space=pl.ANY),
        scratch_shapes=[pltpu.SemaphoreType.DMA])
    return pl.pallas_call(
        functools.partial(_dispatch_kernel, tm=tm, n=n),
        grid_spec=grid_spec,
        out_shape=jax.ShapeDtypeStruct((n_slots, D_MODEL), F32),
        input_output_aliases={2: 0},
        compiler_params=_params("arbitrary"),
        name="dispatch",
    )(dest, x1, jnp.zeros((n_slots, D_MODEL), F32))


def _expert_kernel(be_ref, nu_ref, xs_ref, wg_ref, wu_ref, wd_ref, y_o):
    i = pl.program_id(0)

    @pl.when(i < nu_ref[0])
    def _():
        xb = xs_ref[...].astype(BF16)
        gate = _dot(xb, wg_ref[...])
        h = gate * _sigmoid(gate) * _dot(xb, wu_ref[...])
        y_o[...] = _dot(h.astype(BF16), wd_ref[...])

    @pl.when(i >= nu_ref[0])
    def _():
        y_o[...] = jnp.zeros_like(y_o)


def _experts(block_e, n_used, xs, w_gate, w_up, w_down):
    n_blocks = xs.shape[0] // SLOT_BLOCK
    blk = pl.BlockSpec((SLOT_BLOCK, D_MODEL), lambda i, be, nu: (i, 0))
    grid_spec = pltpu.PrefetchScalarGridSpec(
        num_scalar_prefetch=2,
        grid=(n_blocks,),
        in_specs=[blk,
                  pl.BlockSpec((None, D_MODEL, D_EXPERT), lambda i, be, nu: (be[i], 0, 0)),
                  pl.BlockSpec((None, D_MODEL, D_EXPERT), lambda i, be, nu: (be[i], 0, 0)),
                  pl.BlockSpec((None, D_EXPERT, D_MODEL), lambda i, be, nu: (be[i], 0, 0))],
        out_specs=blk)
    return pl.pallas_call(
        _expert_kernel,
        grid_spec=grid_spec,
        out_shape=jax.ShapeDtypeStruct(xs.shape, F32),
        compiler_params=_params("arbitrary"),
        name="experts",
    )(block_e, n_used, xs, w_gate, w_up, w_down)


def _combine_kernel(dest_ref, x_ref, p_ref, wt_ref, wpg, wple, g_ref, b_ref, yb_ref, x2_o, buf, sem, *, tm, n):
    i = pl.program_id(0)

    def row_copy(r, k):
        return pltpu.make_async_copy(yb_ref.at[pl.ds(dest_ref[k * n + i * tm + r], 1)],
                                     buf.at[k, pl.ds(r, 1)], sem.at[k])

    def issue(r, c):
        row_copy(r, 0).start()
        row_copy(r, 1).start()
        return c

    def drain(r, c):
        row_copy(r, 0).wait()
        row_copy(r, 1).wait()
        return c

    lax.fori_loop(0, tm, issue, 0, unroll=DMA_UNROLL)
    x = x_ref[...]
    ple = _sigmoid(_dot(x.astype(BF16), wpg[...])) * _dot(p_ref[...].astype(BF16), wple[...])
    lax.fori_loop(0, tm, drain, 0, unroll=DMA_UNROLL)
    wt = wt_ref[...]
    y = wt[:, 0:1] * buf[0] + wt[:, 1:2] * buf[1]
    x2_o[...] = _layer_norm(DEEPNORM_ALPHA * x + y + ple, g_ref[...], b_ref[...])


def _combine(dest, x1, p, wt, wpg, wple, g, b, yb, tm):
    n = x1.shape[0]
    row = lambda i, d: (i, 0)
    grid_spec = pltpu.PrefetchScalarGridSpec(
        num_scalar_prefetch=1,
        grid=(n // tm,),
        in_specs=[pl.BlockSpec((tm, D_MODEL), row), pl.BlockSpec((tm, PLE_DIM), row),
                  pl.BlockSpec((tm, SUBLANES), row)]
        + [pl.BlockSpec(a.shape, lambda i, d: (0, 0)) for a in (wpg, wple, g, b)]
        + [pl.BlockSpec(memory_space=pl.ANY)],
        out_specs=pl.BlockSpec((tm, D_MODEL), row),
        scratch_shapes=[pltpu.VMEM((2, tm, D_MODEL), F32), pltpu.SemaphoreType.DMA((2,))])
    return pl.pallas_call(
        functools.partial(_combine_kernel, tm=tm, n=n),
        grid_spec=grid_spec,
        out_shape=jax.ShapeDtypeStruct((n, D_MODEL), F32),
        compiler_params=_params("arbitrary"),
        name="combine",
    )(dest, x1, p, wt, wpg, wple, g, b, yb)


def _tile(n, pref):
    return pref if n % pref == 0 else n


def _rope_tables(pos):
    def tab(half, reps):
        freq = ROPE_THETA ** (-jnp.arange(half, dtype=F32) / half)
        ang = pos.astype(F32)[:, None] * freq[None, :]
        cos, sin = jnp.cos(ang), jnp.sin(ang)
        return jnp.tile(jnp.concatenate([cos, cos], -1), (1, reps)), jnp.tile(jnp.concatenate([-sin, sin], -1), (1, reps))
    return tab(HEAD_DIM // 2, 1) + tab(IDX_DIM // 2, 2)


def _split_w_in(w_in):
    sizes = (N_HEADS * HEAD_DIM, N_KV_HEADS * HEAD_DIM, N_KV_HEADS * HEAD_DIM, N_IDX_HEADS * IDX_DIM,
             N_IDX_HEADS, IDX_DIM, D_MODEL, D_MODEL, D_MODEL, 3 * D_MODEL)
    parts, o = [], 0
    for s in sizes:
        parts.append(w_in[:, o:o + s].astype(BF16))
        o += s
    wq, wk, wv, wiq, wiw, wik, wxr, wgr, wxq, wgz = parts
    wiw = jnp.pad(wiw, ((0, 0), (0, LANES - N_IDX_HEADS)))
    wik = jnp.concatenate([wik, wik], axis=1)
    return (wq, wk, wv, wiq, wiw, wik), (wxr, wgr, wxq), wgz


def _ffn(x1, p, lw):
    n = x1.shape[0]
    tm = _tile(n, 512)
    ei, wt, counts = _router(x1, lw["w_rt"], lw["b_rt"], tm)
    counts = counts[:, 0].astype(I32)
    padded = (counts + SLOT_BLOCK - 1) // SLOT_BLOCK * SLOT_BLOCK
    pad_end = jnp.cumsum(padded)
    pad_start = pad_end - padded
    experts = jnp.arange(N_EXPERTS, dtype=I32)[:, None, None]
    start = jnp.sum(jnp.where(ei[None, 0:2] == experts, pad_start[:, None, None], 0), axis=0)
    dest = (start + ei[2:4]).reshape(-1)
    n_blocks = -(-2 * n // SLOT_BLOCK) + N_EXPERTS
    blk_start = jnp.arange(n_blocks, dtype=I32) * SLOT_BLOCK
    block_e = jnp.minimum(jnp.sum((pad_end[None, :] <= blk_start[:, None]).astype(I32), axis=1), N_EXPERTS - 1)
    n_used = (pad_end[-1:] // SLOT_BLOCK).astype(I32)
    xs = _dispatch(dest, x1, n_blocks * SLOT_BLOCK, tm)
    yb = _experts(block_e, n_used, xs, lw["w_gate"], lw["w_up"], lw["w_down"])
    tc = _tile(n, 256)
    return _combine(dest, x1, p, wt.T, lw["w_ple_gate"], lw["w_ple"], lw["ln2_g"], lw["ln2_b"], yb, tc)


def kernel(x_prompt, x_sample, p_prompt, p_sample, cache_k, cache_v, cache_idx, state_lru_h, state_lru_conv, state_pool, page_table, w_in, w_out, lru_conv_w, lru_conv_b, lru_wa, lru_ba, lru_wi, lru_bi, lru_lambda, pool_w, pool_scale, ln1_g, ln1_b, w_router_group, b_router_group, w_router_expert, b_router_expert, w_exp_gate, w_exp_up, w_exp_down, w_ple, w_ple_gate, ln2_g, ln2_b):
    bp, tp = x_prompt.shape[:2]
    bs, ts = x_sample.shape[:2]
    assert ts == 1, "the sample group decodes one token per sequence"
    depth = w_in.shape[0]
    past = page_table.shape[1] * PAGE_SIZE
    n_p = bp * tp
    tm_p = _tile(tp, 512)
    tq = _tile(tp, 128)
    tt = _tile(tp, 256)
    tabs_p = _rope_tables(jnp.arange(tp))
    tabs_s = _rope_tables(jnp.full((bs,), past, I32))
    cache_idx_t = jnp.swapaxes(cache_idx, 2, 3)

    xp = x_prompt.reshape(n_p, D_MODEL)
    xs = x_sample.reshape(bs, D_MODEL)
    outs = [[] for _ in range(12)]
    row2 = lambda a: a.reshape(1, -1)
    for i in range(depth):
        w_attn, w_mix, wgz = _split_w_in(w_in[i])
        mix_w = w_mix + (lru_conv_w[i], row2(lru_conv_b[i]), lru_wa[i].astype(BF16), row2(lru_ba[i]),
                         lru_wi[i].astype(BF16), row2(lru_bi[i]), row2(lru_lambda[i]),
                         pool_w[i].astype(BF16), row2(pool_scale[i]))
        wout = w_out[i].astype(BF16)
        g1, b1 = row2(ln1_g[i]), row2(ln1_b[i])
        w_rt = jnp.concatenate([w_router_expert[i].T, w_router_group[i].T,
                                jnp.zeros((ROUTER_ROWS - N_EXPERTS - N_EXPERT_GROUPS, D_MODEL), F32)], 0)
        b_rt = jnp.concatenate([b_router_expert[i], b_router_group[i],
                                jnp.zeros((ROUTER_ROWS - N_EXPERTS - N_EXPERT_GROUPS,), F32)])[:, None]
        lw = dict(w_rt=w_rt, b_rt=b_rt, w_gate=w_exp_gate[i].astype(BF16), w_up=w_exp_up[i].astype(BF16),
                  w_down=w_exp_down[i].astype(BF16), w_ple_gate=w_ple_gate[i].astype(BF16),
                  w_ple=w_ple[i].astype(BF16), ln2_g=row2(ln2_g[i]), ln2_b=row2(ln2_b[i]))

        q, k, v, kb, vb, iq, iw, ik = _proj_attn(xp, w_attn, tabs_p, tm_p, tp // tm_p)
        y_att = _dsa_prompt(q, kb, vb, iq, iw, ik, bp, tp, tq)
        y_lru, y_pool, h_new, conv_new, pool_new = _mix_prompt(xp, mix_w, bp, tp, tt)
        x1 = _merge(xp, y_att, y_lru, y_pool, wgz, wout, g1, b1, _tile(n_p, 256))
        xp = _ffn(x1, p_prompt[i].reshape(n_p, PLE_DIM), lw)
        outs[0].append(k.reshape(bp, tp, N_KV_HEADS, HEAD_DIM))
        outs[1].append(v.reshape(bp, tp, N_KV_HEADS, HEAD_DIM))
        outs[2].append(ik[:, :IDX_DIM].reshape(bp, tp, IDX_DIM))
        outs[3].append(h_new[:, 0])
        outs[4].append(conv_new[:, SUBLANES - (CONV_WIDTH - 1):])
        outs[5].append(pool_new[:, 2 * SUBLANES - POOL_BUF:])

        q, k, v, _, _, iq, iw, ik = _proj_attn(xs, w_attn, tabs_s, bs, 1)
        sel_idx = _dsa_sample_select(page_table, iq, iw, ik[:, :IDX_DIM], cache_idx_t, i)
        y_att = _dsa_sample_attend(sel_idx[:, :, 0], page_table, q, k, v, cache_k, cache_v, i)
        y_lru, y_pool, h_new, xr, xq = _mix_sample(xs, mix_w, state_lru_conv[i], state_lru_h[i], state_pool[i], past)
        x1 = _merge(xs, y_att, y_lru, y_pool, wgz, wout, g1, b1, bs)
        xs = _ffn(x1, p_sample[i].reshape(bs, PLE_DIM), lw)
        outs[6].append(k.reshape(bs, ts, N_KV_HEADS, HEAD_DIM))
        outs[7].append(v.reshape(bs, ts, N_KV_HEADS, HEAD_DIM))
        outs[8].append(ik[:, :IDX_DIM].reshape(bs, ts, IDX_DIM))
        outs[9].append(h_new)
        outs[10].append(jnp.concatenate([state_lru_conv[i][:, 1:], xr[:, None]], 1))
        outs[11].append(jnp.concatenate([state_pool[i][:, 1:], xq[:, None]], 1))

    return (xp.reshape(bp, tp, D_MODEL), xs.reshape(bs, ts, D_MODEL)) + tuple(jnp.stack(o) for o in outs)
```

```python
import functools

import jax
import jax.numpy as jnp
import numpy as np
from jax import lax
from jax.experimental import pallas as pl
from jax.experimental.pallas import tpu as pltpu

F32 = jnp.float32
BF16 = jnp.bfloat16
I32 = jnp.int32

D_MODEL = 1024
N_HEADS = 8
HEAD_DIM = 128
N_KV_HEADS = 4
KV_GROUP = N_HEADS // N_KV_HEADS
N_IDX_HEADS = 8
IDX_DIM = 64
INDEX_SCALE = (IDX_DIM * N_IDX_HEADS) ** -0.5
TOPK_MAX = 256
ROPE_THETA = 10000.0
PAGE_SIZE = 128
N_LRU_BLOCKS = 8
LRU_BLOCK = D_MODEL // N_LRU_BLOCKS
CONV_WIDTH = 4
LRU_C = 8.0
POOL_WINDOWS = (2, 4, 8, 16)
POOL_GROUP = D_MODEL // len(POOL_WINDOWS)
POOL_BUF = max(POOL_WINDOWS) - 1
N_EXPERT_GROUPS = 4
EXPERTS_PER_GROUP = 8
N_EXPERTS = N_EXPERT_GROUPS * EXPERTS_PER_GROUP
D_EXPERT = 512
PLE_DIM = 256
LN_EPS = 1e-5
DEPTH = 2
DEEPNORM_ALPHA = (2 * DEPTH) ** 0.25
ATTN_SCALE = HEAD_DIM ** -0.5

LANES = 128
SUBLANES = 8
SLOT_BLOCK = 256
DSA_EXTENTS = 8
DSA_Q_BLOCK = 256
RADIX4_MAX_ELEMS = 256 * 1024
SEQ_GROUP = 8
DMA_UNROLL = 8
VMEM_LIMIT = 48 * 1024 * 1024
INT_MIN = -2 ** 31
KEY_NEG_INF = INT_MIN + 0x7FFFFF

_NT = (((1,), (1,)), ((), ()))


def _dot(a, b):
    return jnp.dot(a, b, preferred_element_type=F32)


def _dot_nt(a, b):
    return lax.dot_general(a, b, _NT, preferred_element_type=F32)


def _sigmoid(x):
    return 1.0 / (1.0 + jnp.exp(-x))


def _gelu_tanh(x):
    c = np.float32(np.sqrt(2.0 / np.pi))
    return x * (0.5 * (1.0 + jnp.tanh(c * (x + 0.044715 * (x * x * x)))))


def _softplus(x):
    return jnp.maximum(x, 0.0) + jnp.log1p(jnp.exp(-jnp.abs(x)))


def _layer_norm(r, g, b):
    mu = jnp.mean(r, axis=-1, keepdims=True)
    c = r - mu
    var = jnp.mean(c * c, axis=-1, keepdims=True)
    return c * lax.rsqrt(var + LN_EPS) * g + b


def _sort_key(x):
    bits = pltpu.bitcast(jnp.where(x == 0.0, 0.0, x), I32)
    return jnp.where(bits >= 0, bits, bits ^ 0x7FFFFFFF)


def _params(*sem):
    return pltpu.CompilerParams(dimension_semantics=sem, vmem_limit_bytes=VMEM_LIMIT)


def _full(shape):
    n = len(shape)
    return pl.BlockSpec(shape, lambda *_: (0,) * n)


def _proj_attn_kernel(x_ref, wq, wk, wv, wiq, wiw, wik, cos_ref, sin_ref, cosi_ref, sini_ref,
                      q_o, k_o, v_o, kb_o, vb_o, iq_o, iw_o, ik_o):
    xb = x_ref[...].astype(BF16)
    cos, sin = cos_ref[...], sin_ref[...]
    cosi, sini = cosi_ref[...], sini_ref[...]
    lane = lax.broadcasted_iota(I32, cos.shape, 1)
    low_half = (lane & (IDX_DIM - 1)) < (IDX_DIM // 2)

    def rope128(z):
        return z * cos + pltpu.roll(z, HEAD_DIM // 2, 1) * sin

    def rope64(z):
        partner = jnp.where(low_half, pltpu.roll(z, LANES - IDX_DIM // 2, 1), pltpu.roll(z, IDX_DIM // 2, 1))
        return z * cosi + partner * sini

    q = _dot(xb, wq[...])
    for h in range(N_HEADS):
        sl = slice(h * LANES, (h + 1) * LANES)
        q_o[:, sl] = (rope128(q[:, sl]) * ATTN_SCALE).astype(BF16)
    k = _dot(xb, wk[...])
    for c in range(N_KV_HEADS):
        sl = slice(c * LANES, (c + 1) * LANES)
        kr = rope128(k[:, sl])
        k_o[:, sl] = kr
        kb_o[:, sl] = kr.astype(BF16)
    v = _dot(xb, wv[...])
    v_o[...] = v
    vb_o[...] = v.astype(BF16)
    iq = _dot(xb, wiq[...])
    for j in range(N_IDX_HEADS * IDX_DIM // LANES):
        sl = slice(j * LANES, (j + 1) * LANES)
        iq_o[:, sl] = rope64(iq[:, sl])
    iw_o[...] = _dot(xb, wiw[...])
    ik_o[...] = rope64(_dot(xb, wik[...]))


def _proj_attn(x, w, tabs, tm, n_tab_blocks):
    n = x.shape[0]
    cos, sin, cosi, sini = tabs
    row = lambda i: (i, 0)
    tab = lambda i: (i % n_tab_blocks, 0)
    wspec = lambda a: _full(a.shape)
    outs = [
        ((n, 1024), BF16), ((n, 512), F32), ((n, 512), F32), ((n, 512), BF16), ((n, 512), BF16),
        ((n, 512), F32), ((n, LANES), F32), ((n, LANES), F32)]
    return pl.pallas_call(
        _proj_attn_kernel,
        grid=(n // tm,),
        in_specs=[pl.BlockSpec((tm, D_MODEL), row)] + [wspec(a) for a in w]
        + [pl.BlockSpec((tm, LANES), tab)] * 4,
        out_specs=[pl.BlockSpec((tm, s[1]), row) for s, _ in outs],
        out_shape=[jax.ShapeDtypeStruct(s, d) for s, d in outs],
        compiler_params=_params("parallel"),
        name="proj_attn",
    )(x, *w, cos, sin, cosi, sini)


def _count_rows(mask):
    return jnp.sum(jnp.where(mask, 1.0, 0.0), axis=1, keepdims=True)


def _dsa_prompt_kernel(q_ref, kb_ref, vb_ref, iq_ref, iw_ref, ik_ref, *rest, tq, t_len, q0, topk, idx_bits):
    o_ref, p_sc = rest[-2:]
    qi = q0 + pl.program_id(1)
    ikb = ik_ref[...].astype(BF16)
    iq = iq_ref[...]
    iw = iw_ref[...] * INDEX_SCALE
    lane = lax.broadcasted_iota(I32, (tq, LANES), 1)
    score = jnp.zeros((tq, t_len), F32)
    for h in range(N_IDX_HEADS):
        chunk = iq[:, (h // 2) * LANES:(h // 2 + 1) * LANES]
        keep = (lane < IDX_DIM) if h % 2 == 0 else (lane >= IDX_DIM)
        s = _dot_nt(jnp.where(keep, chunk, 0.0).astype(BF16), ikb)
        score = score + jnp.maximum(s, 0.0) * iw[:, h:h + 1]
    qpos = qi * tq + lax.broadcasted_iota(I32, (tq, 1), 0)
    kpos = lax.broadcasted_iota(I32, (tq, t_len), 1)
    causal = kpos <= qpos
    key = _sort_key(jnp.where(causal, score, -jnp.inf))

    if tq * t_len <= RADIX4_MAX_ELEMS:
        def thr_body(b, t):
            step = lax.shift_left(jnp.int32(1), 30 - 2 * b)
            cands = [t + step, t + 2 * step, t + 3 * step]
            ok = [_count_rows(key >= c) >= topk for c in cands]
            return jnp.where(ok[2], cands[2], jnp.where(ok[1], cands[1], jnp.where(ok[0], cands[0], t)))
        n_steps = 16
    else:
        def thr_body(b, t):
            cand = t + lax.shift_left(jnp.int32(1), 31 - b)
            return jnp.where(_count_rows(key >= cand) >= topk, cand, t)
        n_steps = 32

    thr = lax.fori_loop(0, n_steps, thr_body, jnp.full((tq, 1), INT_MIN, I32))
    gt = key > thr
    eq = key == thr
    need = topk - _count_rows(gt)
    tie = (_count_rows(eq) > need) & (thr > KEY_NEG_INF)
    p_sc[...] = jnp.full((tq, 1), t_len, I32)

    @pl.when(jnp.max(jnp.where(tie, 1.0, 0.0)) > 0.0)
    def _():
        def pos_body(b, p):
            cand = p + lax.shift_left(jnp.int32(1), idx_bits - 1 - b)
            return jnp.where(_count_rows(eq & (kpos < cand)) < need, cand, p)
        p_sc[...] = lax.fori_loop(0, idx_bits, pos_body, jnp.zeros((tq, 1), I32))

    sel = causal & (gt | (eq & (kpos <= p_sc[...])))
    bias = jnp.where(sel, 0.0, -jnp.inf)
    for c in range(N_KV_HEADS):
        kc = kb_ref[:, c * LANES:(c + 1) * LANES]
        vc = vb_ref[:, c * LANES:(c + 1) * LANES]
        for g in range(KV_GROUP):
            h = c * KV_GROUP + g
            s = _dot_nt(q_ref[:, h * LANES:(h + 1) * LANES], kc) + bias
            m = jnp.max(s, axis=1, keepdims=True)
            p = jnp.exp(s - m)
            l = jnp.sum(p, axis=1, keepdims=True)
            o_ref[:, h * LANES:(h + 1) * LANES] = _dot(p.astype(BF16), vc) / l


def _dsa_prompt(q, kb, vb, iq, iw, ik, batch, t_len, tq, y_init=None):
    topk = min(TOPK_MAX, t_len // 4)
    nq = t_len // tq
    n_var = max(d for d in range(1, DSA_EXTENTS + 1) if nq % d == 0)
    per = nq // n_var
    kb, vb, ik = (a.reshape(batch, t_len, a.shape[-1]) for a in (kb, vb, ik))
    y = jnp.zeros((batch * t_len, 1024), F32) if y_init is None else y_init
    for v in range(n_var):
        ext = (v + 1) * per * tq
        qrow = lambda b, i, v=v: (b * nq + v * per + i, 0)
        brow = lambda b, i: (b, 0, 0)
        in_specs = [pl.BlockSpec((tq, 1024), qrow), pl.BlockSpec((None, ext, 512), brow),
                    pl.BlockSpec((None, ext, 512), brow), pl.BlockSpec((tq, 512), qrow),
                    pl.BlockSpec((tq, LANES), qrow), pl.BlockSpec((None, ext, LANES), brow)]
        in_specs.append(pl.BlockSpec(memory_space=pl.ANY))
        y = pl.pallas_call(
            functools.partial(_dsa_prompt_kernel, tq=tq, t_len=ext, q0=v * per, topk=topk,
                              idx_bits=max(1, int(np.ceil(np.log2(ext))))),
            grid=(batch, per),
            in_specs=in_specs,
            out_specs=pl.BlockSpec((tq, 1024), qrow),
            out_shape=jax.ShapeDtypeStruct((batch * t_len, 1024), F32),
            scratch_shapes=[pltpu.VMEM((tq, 1), I32)],
            input_output_aliases={6: 0},
            compiler_params=_params("parallel", "arbitrary"),
            name="dsa_prompt",
        )(q, kb, vb, iq, iw, ik, y)
    return y


def _dsa_sample_select_kernel(pt_ref, iq_ref, iw_ref, ikn_ref, cidx_ref, idx_o,
                              ikbuf, s_sc, c_sc, sem, *, layer, n_pages, topk, idx_bits, group):
    g0 = pl.program_id(0) * group
    past = n_pages * PAGE_SIZE

    def page_copy(s, j):
        slot = s % 2
        return pltpu.make_async_copy(cidx_ref.at[layer, pt_ref[(g0 + s) * n_pages + j]],
                                     ikbuf.at[slot, j], sem.at[slot])

    def issue_seq(s):
        def body(j, c):
            page_copy(s, j).start()
            return c
        lax.fori_loop(0, n_pages, body, 0, unroll=DMA_UNROLL)

    def drain_seq(s):
        def body(j, c):
            page_copy(s, j).wait()
            return c
        lax.fori_loop(0, n_pages, body, 0, unroll=DMA_UNROLL)

    issue_seq(0)

    def seq_body(s, c):
        @pl.when(s + 1 < group)
        def _():
            issue_seq(s + 1)
        drain_seq(s)
        slot = s % 2
        iq8 = iq_ref[s].astype(BF16)
        w8 = iw_ref[s] * INDEX_SCALE
        for j in range(n_pages):
            s8 = _dot(iq8, ikbuf[slot, j].astype(BF16))
            s_sc[s, j:j + 1, :] = jnp.sum(jnp.maximum(s8, 0.0) * w8, axis=0, keepdims=True)
        return c

    lax.fori_loop(0, group, seq_body, 0)

    w_all = iw_ref[...] * INDEX_SCALE
    own = jnp.sum(iq_ref[...].astype(BF16).astype(F32) * ikn_ref[...].astype(BF16).astype(F32),
                  axis=2, keepdims=True)
    own = jnp.sum(jnp.maximum(own, 0.0) * w_all, axis=1, keepdims=True)

    key = _sort_key(s_sc[...])
    key_own = _sort_key(own)
    pos = (lax.broadcasted_iota(I32, key.shape, 1) * PAGE_SIZE + lax.broadcasted_iota(I32, key.shape, 2))

    def count(mask, mask_own):
        c = jnp.sum(jnp.where(mask, 1.0, 0.0), axis=1, keepdims=True)
        return jnp.sum(c, axis=2, keepdims=True) + jnp.where(mask_own, 1.0, 0.0)

    def thr_body(i, t):
        cand = t + lax.shift_left(jnp.int32(1), 31 - i)
        return jnp.where(count(key >= cand, key_own >= cand) >= topk, cand, t)

    thr = lax.fori_loop(0, 32, thr_body, jnp.full((group, 1, 1), INT_MIN, I32))
    gt, eq = key > thr, key == thr
    gt_own, eq_own = key_own > thr, key_own == thr
    need = topk - count(gt, gt_own)

    def pos_body(i, p):
        cand = p + lax.shift_left(jnp.int32(1), idx_bits - 1 - i)
        return jnp.where(count(eq & (pos < cand), eq_own & (past < cand)) < need, cand, p)

    plast = lax.fori_loop(0, idx_bits, pos_body, jnp.zeros((group, 1, 1), I32))
    s_sc[...] = jnp.where(gt | (eq & (pos <= plast)), 1.0, 0.0)

    r_i = lax.broadcasted_iota(I32, (PAGE_SIZE, PAGE_SIZE), 0)
    c_i = lax.broadcasted_iota(I32, (PAGE_SIZE, PAGE_SIZE), 1)
    tri_incl = jnp.where(r_i <= c_i, 1.0, 0.0).astype(BF16)
    pr = lax.broadcasted_iota(I32, (n_pages, n_pages), 0)
    pc = lax.broadcasted_iota(I32, (n_pages, n_pages), 1)
    tri_pages = jnp.where(pc < pr, 1.0, 0.0).astype(BF16)
    rank = lax.broadcasted_iota(I32, (topk, PAGE_SIZE), 0).astype(F32)

    def compact(s, c):
        within = _dot(s_sc[s].astype(BF16), tri_incl)
        tot = jnp.broadcast_to(within[:, PAGE_SIZE - 1:PAGE_SIZE], (n_pages, PAGE_SIZE)).astype(BF16)
        c_sc[...] = within + _dot(tri_pages, tot)

        def acc_body(j, acc):
            return acc + jnp.where(c_sc[pl.ds(j, 1), :] <= rank, 1.0, 0.0)

        acc = lax.fori_loop(0, n_pages, acc_body, jnp.zeros((topk, PAGE_SIZE), F32))
        idx_o[s] = jnp.sum(acc, axis=1, keepdims=True).astype(I32)
        return c

    lax.fori_loop(0, group, compact, 0)


def _dsa_sample_select(page_table, iq, iw, ik_new, cache_idx_t, layer):
    bd, n_pages = page_table.shape
    past = n_pages * PAGE_SIZE
    topk = min(TOPK_MAX, (past + 1) // 4)
    idx_bits = int(np.floor(np.log2(past))) + 1
    group = SEQ_GROUP if bd % SEQ_GROUP == 0 else bd
    grp = lambda g, pt: (g, 0, 0)
    grid_spec = pltpu.PrefetchScalarGridSpec(
        num_scalar_prefetch=1,
        grid=(bd // group,),
        in_specs=[pl.BlockSpec((group, N_IDX_HEADS, IDX_DIM), grp),
                  pl.BlockSpec((group, N_IDX_HEADS, 1), grp),
                  pl.BlockSpec((group, 1, IDX_DIM), grp),
                  pl.BlockSpec(memory_space=pl.ANY)],
        out_specs=pl.BlockSpec((group, topk, 1), grp),
        scratch_shapes=[pltpu.VMEM((2, n_pages, IDX_DIM, PAGE_SIZE), F32), pltpu.VMEM((group, n_pages, PAGE_SIZE), F32),
                        pltpu.VMEM((n_pages, PAGE_SIZE), F32), pltpu.SemaphoreType.DMA((2,))])
    return pl.pallas_call(
        functools.partial(_dsa_sample_select_kernel, layer=layer, n_pages=n_pages, topk=topk, idx_bits=idx_bits,
                          group=group),
        grid_spec=grid_spec,
        out_shape=jax.ShapeDtypeStruct((bd, topk, 1), I32),
        compiler_params=_params("arbitrary"),
        name="dsa_sample_select",
    )(page_table.reshape(-1), iq.reshape(bd, N_IDX_HEADS, IDX_DIM), iw[:, :N_IDX_HEADS, None],
      ik_new[:, None, :], cache_idx_t)


def _dsa_sample_attend_kernel(idx_ref, pt_ref, q_ref, kn_ref, vn_ref, idxv_ref, ck_ref, cv_ref, o_ref,
                              kbuf, vbuf, sem, *, layer, n_pages, topk):
    b = pl.program_id(0)
    past = n_pages * PAGE_SIZE

    def row_copies(r):
        pidx = jnp.minimum(idx_ref[b * topk + r], past - 1)
        phys = pt_ref[b * n_pages + pidx // PAGE_SIZE]
        off = pidx % PAGE_SIZE
        return (pltpu.make_async_copy(ck_ref.at[layer, phys, off], kbuf.at[r], sem.at[0]),
                pltpu.make_async_copy(cv_ref.at[layer, phys, off], vbuf.at[r], sem.at[1]))

    def issue(r, c):
        for cp in row_copies(r):
            cp.start()
        return c

    def drain(r, c):
        for cp in row_copies(r):
            cp.wait()
        return c

    lax.fori_loop(0, topk, issue, 0, unroll=DMA_UNROLL)
    lax.fori_loop(0, topk, drain, 0, unroll=DMA_UNROLL)

    own = idxv_ref[...] >= past
    k_sel = jnp.where(own, kn_ref[...][None], kbuf[...])
    v_sel = jnp.where(own, vn_ref[...][None], vbuf[...])
    for g in range(KV_GROUP):
        s = jnp.sum(k_sel * q_ref[g][None], axis=-1, keepdims=True)
        m = jnp.max(s, axis=0, keepdims=True)
        p = jnp.exp(s - m)
        l = jnp.sum(p, axis=0)
        o_ref[g] = jnp.sum(p * v_sel, axis=0) / l


def _dsa_sample_attend(idx, page_table, q, k_new, v_new, cache_k, cache_v, layer):
    bd, n_pages = page_table.shape
    topk = idx.shape[1]
    qg = q.astype(F32).reshape(bd, N_KV_HEADS, KV_GROUP, HEAD_DIM).transpose(0, 2, 1, 3)
    head = lambda b, *_: (b, 0, 0)
    grid_spec = pltpu.PrefetchScalarGridSpec(
        num_scalar_prefetch=2,
        grid=(bd,),
        in_specs=[pl.BlockSpec((None, KV_GROUP, N_KV_HEADS, HEAD_DIM), lambda b, *_: (b, 0, 0, 0)),
                  pl.BlockSpec((None, N_KV_HEADS, HEAD_DIM), head),
                  pl.BlockSpec((None, N_KV_HEADS, HEAD_DIM), head),
                  pl.BlockSpec((None, topk, 1, 1), lambda b, *_: (b, 0, 0, 0)),
                  pl.BlockSpec(memory_space=pl.ANY), pl.BlockSpec(memory_space=pl.ANY)],
        out_specs=pl.BlockSpec((None, KV_GROUP, N_KV_HEADS, HEAD_DIM), lambda b, *_: (b, 0, 0, 0)),
        scratch_shapes=[pltpu.VMEM((topk, N_KV_HEADS, HEAD_DIM), F32), pltpu.VMEM((topk, N_KV_HEADS, HEAD_DIM), F32),
                        pltpu.SemaphoreType.DMA((2,))])
    o = pl.pallas_call(
        functools.partial(_dsa_sample_attend_kernel, layer=layer, n_pages=n_pages, topk=topk),
        grid_spec=grid_spec,
        out_shape=jax.ShapeDtypeStruct((bd, KV_GROUP, N_KV_HEADS, HEAD_DIM), F32),
        compiler_params=_params("arbitrary"),
        name="dsa_sample_attend",
    )(idx.reshape(-1), page_table.reshape(-1), qg, k_new.reshape(bd, N_KV_HEADS, HEAD_DIM),
      v_new.reshape(bd, N_KV_HEADS, HEAD_DIM), idx.reshape(bd, topk, 1, 1), cache_k, cache_v)
    return o.transpose(0, 2, 1, 3).reshape(bd, N_HEADS * HEAD_DIM)


def _lru_gates(xc, wa, ba, wi, bi, lam):
    xcb = xc.astype(BF16)
    a_parts, u_parts = [], []
    for n in range(N_LRU_BLOCKS):
        sl = slice(n * LRU_BLOCK, (n + 1) * LRU_BLOCK)
        r = _sigmoid(_dot(xcb[:, sl], wa[n]) + ba[:, sl])
        i = _sigmoid(_dot(xcb[:, sl], wi[n]) + bi[:, sl])
        log_a = -LRU_C * r * _softplus(-lam[:, sl])
        a_parts.append(jnp.exp(log_a))
        th = jnp.tanh(log_a)
        u_parts.append(jnp.sqrt(-2.0 * th / (1.0 - th)) * i * xc[:, sl])
    return a_parts, u_parts


def _pool_mix(window_sum, xq, cnt, pool_w, pool_scale, g):
    sl = slice(g * POOL_GROUP, (g + 1) * POOL_GROUP)
    pooled = window_sum / cnt - xq[:, sl]
    return _dot(pooled.astype(BF16), pool_w[g]) * pool_scale[:, sl]


def _mix_prompt_kernel(x_ref, wxr, wgr, wxq, convw, convb, wa, ba, wi, bi, lam, poolw, pscale,
                       ylru_o, ypool_o, h_o, conv_o, pool_o,
                       xr_ext, xq_ext, a0, u0, a1, u1, h_sc, *, tt, pad):
    t = pl.program_id(1)
    halo_r, halo_q = SUBLANES, 2 * SUBLANES

    @pl.when(t == 0)
    def _():
        xr_ext[0:halo_r] = jnp.zeros((halo_r, D_MODEL), F32)
        xq_ext[0:halo_q] = jnp.zeros((halo_q, D_MODEL), F32)
        h_sc[...] = jnp.zeros_like(h_sc)
        a0[0:pad] = jnp.ones((pad, D_MODEL), F32)
        a1[0:pad] = jnp.ones((pad, D_MODEL), F32)
        u0[0:pad] = jnp.zeros((pad, D_MODEL), F32)
        u1[0:pad] = jnp.zeros((pad, D_MODEL), F32)

    @pl.when(t > 0)
    def _():
        xr_ext[0:halo_r] = xr_ext[tt:tt + halo_r]
        xq_ext[0:halo_q] = xq_ext[tt:tt + halo_q]

    xb = x_ref[...].astype(BF16)
    xr = _dot(xb, wxr[...])
    xq = _dot(xb, wxq[...])
    xr_ext[halo_r:halo_r + tt] = xr
    xq_ext[halo_q:halo_q + tt] = xq

    cw = convw[...]
    xc = convb[...] + cw[CONV_WIDTH - 1:CONV_WIDTH] * xr
    for j in range(CONV_WIDTH - 1):
        o = halo_r - (CONV_WIDTH - 1) + j
        xc = xc + cw[j:j + 1] * xr_ext[o:o + tt]
    a_parts, u_parts = _lru_gates(xc, wa, ba[...], wi, bi[...], lam[...])
    for n in range(N_LRU_BLOCKS):
        sl = slice(n * LRU_BLOCK, (n + 1) * LRU_BLOCK)
        a0[pad:pad + tt, sl] = a_parts[n]
        u0[pad:pad + tt, sl] = u_parts[n]

    bufs = ((a0, u0), (a1, u1))
    d, cur = 1, 0
    while d < tt:
        (sa, su), (da, du) = bufs[cur], bufs[1 - cur]
        a_cur, u_cur = sa[pad:pad + tt], su[pad:pad + tt]
        du[pad:pad + tt] = u_cur + a_cur * su[pad - d:pad - d + tt]
        da[pad:pad + tt] = a_cur * sa[pad - d:pad - d + tt]
        d, cur = 2 * d, 1 - cur
    sa, su = bufs[cur]
    h = sa[pad:pad + tt] * h_sc[...] + su[pad:pad + tt]
    h_sc[...] = h[tt - 1:tt]
    ylru_o[...] = h * _gelu_tanh(_dot(xb, wgr[...]))

    posn = t * tt + lax.broadcasted_iota(I32, (tt, 1), 0)
    for g, w in enumerate(POOL_WINDOWS):
        sl = slice(g * POOL_GROUP, (g + 1) * POOL_GROUP)
        acc = xq_ext[:, sl]
        d = 1
        while d < w:
            acc = acc + pltpu.roll(acc, d, 0)
            d *= 2
        cnt = jnp.minimum(posn + 1, w).astype(F32)
        ypool_o[:, sl] = _pool_mix(acc[halo_q:], xq, cnt, poolw, pscale[...], g)

    @pl.when(t == pl.num_programs(1) - 1)
    def _():
        h_o[...] = h[tt - 1:tt]
        conv_o[...] = xr[tt - halo_r:tt]
        pool_o[...] = xq[tt - halo_q:tt]


def _mix_prompt(x, w, batch, t_len, tt):
    nt = t_len // tt
    pad = tt // 2
    row = lambda b, t: (b * nt + t, 0)
    per_b = lambda b, t: (b, 0, 0)
    outs = [((batch * t_len, D_MODEL), F32), ((batch * t_len, D_MODEL), F32),
            ((batch, 1, D_MODEL), F32), ((batch, SUBLANES, D_MODEL), F32), ((batch, 2 * SUBLANES, D_MODEL), F32)]
    return pl.pallas_call(
        functools.partial(_mix_prompt_kernel, tt=tt, pad=pad),
        grid=(batch, nt),
        in_specs=[pl.BlockSpec((tt, D_MODEL), row)] + [_full(a.shape) for a in w],
        out_specs=[pl.BlockSpec((tt, D_MODEL), row), pl.BlockSpec((tt, D_MODEL), row),
                   pl.BlockSpec((None, 1, D_MODEL), per_b), pl.BlockSpec((None, SUBLANES, D_MODEL), per_b),
                   pl.BlockSpec((None, 2 * SUBLANES, D_MODEL), per_b)],
        out_shape=[jax.ShapeDtypeStruct(s, d) for s, d in outs],
        scratch_shapes=[pltpu.VMEM((tt + SUBLANES, D_MODEL), F32), pltpu.VMEM((tt + 2 * SUBLANES, D_MODEL), F32)]
        + [pltpu.VMEM((pad + tt, D_MODEL), F32)] * 4 + [pltpu.VMEM((1, D_MODEL), F32)],
        compiler_params=_params("parallel", "arbitrary"),
        name="mix_prompt",
    )(x, *w)


def _mix_sample_kernel(x_ref, wxr, wgr, wxq, convw, convb, wa, ba, wi, bi, lam, poolw, pscale,
                       conv_ref, h_ref, pool_ref, ylru_o, ypool_o, h_o, xr_o, xq_o, *, cnt_pos):
    xb = x_ref[...].astype(BF16)
    xr = _dot(xb, wxr[...])
    xq = _dot(xb, wxq[...])
    cw = convw[...]
    xc = convb[...] + cw[CONV_WIDTH - 1:CONV_WIDTH] * xr
    for j in range(CONV_WIDTH - 1):
        xc = xc + cw[j:j + 1] * conv_ref[j]
    a_parts, u_parts = _lru_gates(xc, wa, ba[...], wi, bi[...], lam[...])
    gate = _gelu_tanh(_dot(xb, wgr[...]))
    for n in range(N_LRU_BLOCKS):
        sl = slice(n * LRU_BLOCK, (n + 1) * LRU_BLOCK)
        h = a_parts[n] * h_ref[:, sl] + u_parts[n]
        h_o[:, sl] = h
        ylru_o[:, sl] = h * gate[:, sl]
    for g, w in enumerate(POOL_WINDOWS):
        sl = slice(g * POOL_GROUP, (g + 1) * POOL_GROUP)
        acc = xq[:, sl]
        for j in range(1, w):
            acc = acc + pool_ref[POOL_BUF - j, :, sl]
        ypool_o[:, sl] = _pool_mix(acc, xq, float(min(cnt_pos, w)), poolw, pscale[...], g)
    xr_o[...] = xr
    xq_o[...] = xq


def _mix_sample(x, w, conv_state, h_state, pool_state, past):
    bd = x.shape[0]
    args = (x, *w, conv_state.transpose(1, 0, 2), h_state, pool_state.transpose(1, 0, 2))
    shp = jax.ShapeDtypeStruct((bd, D_MODEL), F32)
    return pl.pallas_call(
        functools.partial(_mix_sample_kernel, cnt_pos=past + 1),
        in_specs=[_full(a.shape) for a in args],
        out_specs=[_full((bd, D_MODEL))] * 5,
        out_shape=[shp] * 5,
        grid=(1,),
        compiler_params=_params("arbitrary"),
        name="mix_sample",
    )(*args)


def _merge_kernel(x_ref, ya_ref, yl_ref, yp_ref, wgz, wout, g_ref, b_ref, x1_o):
    x = x_ref[...]
    gz = _dot(x.astype(BF16), wgz[...])
    merged = (_sigmoid(gz[:, 0:D_MODEL]) * ya_ref[...] + _sigmoid(gz[:, D_MODEL:2 * D_MODEL]) * yl_ref[...]
              + _sigmoid(gz[:, 2 * D_MODEL:3 * D_MODEL]) * yp_ref[...])
    r = DEEPNORM_ALPHA * x + _dot(merged.astype(BF16), wout[...])
    x1_o[...] = _layer_norm(r, g_ref[...], b_ref[...])


def _merge(x, ya, yl, yp, wgz, wout, g, b, tm):
    n = x.shape[0]
    row = pl.BlockSpec((tm, D_MODEL), lambda i: (i, 0))
    return pl.pallas_call(
        _merge_kernel,
        grid=(n // tm,),
        in_specs=[row] * 4 + [_full(wgz.shape), _full(wout.shape), _full(g.shape), _full(b.shape)],
        out_specs=row,
        out_shape=jax.ShapeDtypeStruct((n, D_MODEL), F32),
        compiler_params=_params("parallel"),
        name="merge",
    )(x, ya, yl, yp, wgz, wout, g, b)


ROUTER_ROWS = 40


def _first_argmax(v, n):
    m = jnp.max(v, axis=0, keepdims=True)
    rows = lax.broadcasted_iota(I32, v.shape, 0)
    return m, jnp.min(jnp.where(v == m, rows, n), axis=0, keepdims=True)


def _router_kernel(x_ref, w_ref, b_ref, ei_o, wt_o, cnt_o, carry, *, tm):
    i = pl.program_id(0)

    @pl.when(i == 0)
    def _():
        carry[...] = jnp.zeros_like(carry)

    def split(v):
        hi = v.astype(BF16)
        return hi, (v - hi.astype(F32)).astype(BF16)

    x_hi, x_lo = split(x_ref[...])
    w_hi, w_lo = split(w_ref[...])
    logits = _dot_nt(w_hi, x_hi) + (_dot_nt(w_hi, x_lo) + _dot_nt(w_lo, x_hi)) + b_ref[...]
    le = logits[0:N_EXPERTS]
    lg = logits[N_EXPERTS:N_EXPERTS + N_EXPERT_GROUPS]

    gmax, gidx = _first_argmax(lg, N_EXPERT_GROUPS)
    p_top = 1.0 / jnp.sum(jnp.exp(lg - gmax), axis=0, keepdims=True)
    le_g = jnp.zeros((EXPERTS_PER_GROUP, tm), F32)
    for gi in range(N_EXPERT_GROUPS):
        le_g = le_g + jnp.where(gidx == gi, le[gi * EXPERTS_PER_GROUP:(gi + 1) * EXPERTS_PER_GROUP], 0.0)
    m1, i1 = _first_argmax(le_g, EXPERTS_PER_GROUP)
    rows8 = lax.broadcasted_iota(I32, le_g.shape, 0)
    m2, i2 = _first_argmax(jnp.where(rows8 == i1, -jnp.inf, le_g), EXPERTS_PER_GROUP)
    z = jnp.sum(jnp.exp(le_g - m1), axis=0, keepdims=True)
    p1 = 1.0 / z
    p2 = jnp.exp(m2 - m1) / z
    e1 = gidx * EXPERTS_PER_GROUP + i1
    e2 = gidx * EXPERTS_PER_GROUP + i2

    rows = lax.broadcasted_iota(I32, (N_EXPERTS, tm), 0)
    hit1, hit2 = rows == e1, rows == e2
    onehot = jnp.where(hit1 | hit2, 1.0, 0.0)
    r_i = lax.broadcasted_iota(I32, (tm, tm), 0)
    c_i = lax.broadcasted_iota(I32, (tm, tm), 1)
    before = _dot(onehot.astype(BF16), jnp.where(r_i < c_i, 1.0, 0.0).astype(BF16)) + carry[...]
    carry[...] = carry[...] + jnp.sum(onehot, axis=1, keepdims=True)

    ei_o[0:1, :] = e1
    ei_o[1:2, :] = e2
    ei_o[2:3, :] = jnp.sum(jnp.where(hit1, before, 0.0), axis=0, keepdims=True).astype(I32)
    ei_o[3:4, :] = jnp.sum(jnp.where(hit2, before, 0.0), axis=0, keepdims=True).astype(I32)
    ei_o[4:SUBLANES, :] = jnp.zeros((SUBLANES - 4, tm), I32)
    wt_o[0:1, :] = p1 / (p1 + p2) * p_top
    wt_o[1:2, :] = p2 / (p1 + p2) * p_top
    wt_o[2:SUBLANES, :] = jnp.zeros((SUBLANES - 2, tm), F32)
    cnt_o[...] = carry[...]


def _router(x1, w_rt, b_rt, tm):
    n = x1.shape[0]
    col = pl.BlockSpec((SUBLANES, tm), lambda i: (0, i))
    return pl.pallas_call(
        functools.partial(_router_kernel, tm=tm),
        grid=(n // tm,),
        in_specs=[pl.BlockSpec((tm, D_MODEL), lambda i: (i, 0)), _full(w_rt.shape), _full(b_rt.shape)],
        out_specs=[col, col, _full((N_EXPERTS, 1))],
        out_shape=[jax.ShapeDtypeStruct((SUBLANES, n), I32), jax.ShapeDtypeStruct((SUBLANES, n), F32),
                   jax.ShapeDtypeStruct((N_EXPERTS, 1), F32)],
        scratch_shapes=[pltpu.VMEM((N_EXPERTS, 1), F32)],
        compiler_params=_params("arbitrary"),
        name="router",
    )(x1, w_rt, b_rt)


def _dispatch_kernel(dest_ref, x_ref, xs_in, xs_out, sem, *, tm, n):
    del xs_in
    i = pl.program_id(0)

    def row_copy(r, k):
        return pltpu.make_async_copy(x_ref.at[pl.ds(r, 1)], xs_out.at[pl.ds(dest_ref[k * n + i * tm + r], 1)], sem)

    def issue(r, c):
        row_copy(r, 0).start()
        row_copy(r, 1).start()
        return c

    def drain(r, c):
        row_copy(r, 0).wait()
        row_copy(r, 1).wait()
        return c

    lax.fori_loop(0, tm, issue, 0, unroll=DMA_UNROLL)
    lax.fori_loop(0, tm, drain, 0, unroll=DMA_UNROLL)


def _dispatch(dest, x1, n_slots, tm, xs_init=None):
    n = x1.shape[0]
    if xs_init is None:
        xs_init = jnp.zeros((n_slots, D_MODEL), F32)
    grid_spec = pltpu.PrefetchScalarGridSpec(
        num_scalar_prefetch=1,
        grid=(n // tm,),
        in_specs=[pl.BlockSpec((tm, D_MODEL), lambda i, d: (i, 0)), pl.BlockSpec(memory_space=pl.ANY)],
        out_specs=pl.BlockSpec(memory_space=pl.ANY),
        scratch_shapes=[pltpu.SemaphoreType.DMA])
    return pl.pallas_call(
        functools.partial(_dispatch_kernel, tm=tm, n=n),
        grid_spec=grid_spec,
        out_shape=jax.ShapeDtypeStruct((n_slots, D_MODEL), F32),
        input_output_aliases={2: 0},
        compiler_params=_params("arbitrary"),
        name="dispatch",
    )(dest, x1, xs_init)


def _expert_kernel(be_ref, nu_ref, xs_ref, wg_ref, wu_ref, wd_ref, y_o, wg_b, wu_b, wd_b):
    i = pl.program_id(0)

    @pl.when(i < nu_ref[0])
    def _():
        @pl.when((i == 0) | (be_ref[i] != be_ref[jnp.maximum(i - 1, 0)]))
        def _():
            wg_b[...] = wg_ref[...].astype(BF16)
            wu_b[...] = wu_ref[...].astype(BF16)
            wd_b[...] = wd_ref[...].astype(BF16)

        xb = xs_ref[...].astype(BF16)
        gate = _dot(xb, wg_b[...])
        h = gate * _sigmoid(gate) * _dot(xb, wu_b[...])
        y_o[...] = _dot(h.astype(BF16), wd_b[...])

    @pl.when(i >= nu_ref[0])
    def _():
        y_o[...] = jnp.zeros_like(y_o)


def _experts(block_e, n_used, xs, w_gate, w_up, w_down, layer):
    n_blocks = xs.shape[0] // SLOT_BLOCK
    blk = pl.BlockSpec((SLOT_BLOCK, D_MODEL), lambda i, be, nu: (i, 0))
    wsel = lambda i, be, nu: (layer, be[i], 0, 0)
    grid_spec = pltpu.PrefetchScalarGridSpec(
        num_scalar_prefetch=2,
        grid=(n_blocks,),
        in_specs=[blk,
                  pl.BlockSpec((None, None, D_MODEL, D_EXPERT), wsel),
                  pl.BlockSpec((None, None, D_MODEL, D_EXPERT), wsel),
                  pl.BlockSpec((None, None, D_EXPERT, D_MODEL), wsel)],
        out_specs=blk,
        scratch_shapes=[pltpu.VMEM((D_MODEL, D_EXPERT), BF16), pltpu.VMEM((D_MODEL, D_EXPERT), BF16),
                        pltpu.VMEM((D_EXPERT, D_MODEL), BF16)])
    return pl.pallas_call(
        _expert_kernel,
        grid_spec=grid_spec,
        out_shape=jax.ShapeDtypeStruct(xs.shape, F32),
        compiler_params=_params("arbitrary"),
        name="experts",
    )(block_e, n_used, xs, w_gate, w_up, w_down)


def _combine_kernel(dest_ref, x_ref, p_ref, wt_ref, wpg, wple, g_ref, b_ref, yb_ref, x2_o, buf, sem, *, tm, n):
    i = pl.program_id(0)

    def row_copy(r, k):
        return pltpu.make_async_copy(yb_ref.at[pl.ds(dest_ref[k * n + i * tm + r], 1)],
                                     buf.at[k, pl.ds(r, 1)], sem.at[k])

    def issue(r, c):
        row_copy(r, 0).start()
        row_copy(r, 1).start()
        return c

    def drain(r, c):
        row_copy(r, 0).wait()
        row_copy(r, 1).wait()
        return c

    for r in range(tm):
        issue(r, 0)
    x = x_ref[...]
    ple = _sigmoid(_dot(x.astype(BF16), wpg[...])) * _dot(p_ref[...].astype(BF16), wple[...])
    lax.fori_loop(0, tm, drain, 0, unroll=DMA_UNROLL)
    wt = wt_ref[...]
    y = wt[:, 0:1] * buf[0] + wt[:, 1:2] * buf[1]
    x2_o[...] = _layer_norm(DEEPNORM_ALPHA * x + y + ple, g_ref[...], b_ref[...])


def _combine(dest, x1, p, wt, wpg, wple, g, b, yb, tm):
    n = x1.shape[0]
    row = lambda i, d: (i, 0)
    grid_spec = pltpu.PrefetchScalarGridSpec(
        num_scalar_prefetch=1,
        grid=(n // tm,),
        in_specs=[pl.BlockSpec((tm, D_MODEL), row), pl.BlockSpec((tm, PLE_DIM), row),
                  pl.BlockSpec((tm, SUBLANES), row)]
        + [pl.BlockSpec(a.shape, lambda i, d: (0, 0)) for a in (wpg, wple, g, b)]
        + [pl.BlockSpec(memory_space=pl.ANY)],
        out_specs=pl.BlockSpec((tm, D_MODEL), row),
        scratch_shapes=[pltpu.VMEM((2, tm, D_MODEL), F32), pltpu.SemaphoreType.DMA((2,))])
    return pl.pallas_call(
        functools.partial(_combine_kernel, tm=tm, n=n),
        grid_spec=grid_spec,
        out_shape=jax.ShapeDtypeStruct((n, D_MODEL), F32),
        compiler_params=_params("arbitrary"),
        name="combine",
    )(dest, x1, p, wt, wpg, wple, g, b, yb)


def _tile(n, pref):
    return pref if n % pref == 0 else n


def _rope_tables(pos):
    def tab(half, reps):
        freq = ROPE_THETA ** (-jnp.arange(half, dtype=F32) / half)
        ang = pos.astype(F32)[:, None] * freq[None, :]
        cos, sin = jnp.cos(ang), jnp.sin(ang)
        return jnp.tile(jnp.concatenate([cos, cos], -1), (1, reps)), jnp.tile(jnp.concatenate([-sin, sin], -1), (1, reps))
    return tab(HEAD_DIM // 2, 1) + tab(IDX_DIM // 2, 2)


def _split_w_in(w_in):
    sizes = (N_HEADS * HEAD_DIM, N_KV_HEADS * HEAD_DIM, N_KV_HEADS * HEAD_DIM, N_IDX_HEADS * IDX_DIM,
             N_IDX_HEADS, IDX_DIM, D_MODEL, D_MODEL, D_MODEL, 3 * D_MODEL)
    parts, o = [], 0
    for s in sizes:
        parts.append(w_in[:, o:o + s].astype(BF16))
        o += s
    wq, wk, wv, wiq, wiw, wik, wxr, wgr, wxq, wgz = parts
    wiw = jnp.pad(wiw, ((0, 0), (0, LANES - N_IDX_HEADS)))
    wik = jnp.concatenate([wik, wik], axis=1)
    return (wq, wk, wv, wiq, wiw, wik), (wxr, wgr, wxq), wgz


def _ffn(x1, p, lw, xs_init=None):
    n = x1.shape[0]
    tm = _tile(n, 512)
    ei, wt, counts = _router(x1, lw["w_rt"], lw["b_rt"], tm)
    counts = counts[:, 0].astype(I32)
    padded = (counts + SLOT_BLOCK - 1) // SLOT_BLOCK * SLOT_BLOCK
    pad_end = jnp.cumsum(padded)
    pad_start = pad_end - padded
    experts = jnp.arange(N_EXPERTS, dtype=I32)[:, None, None]
    start = jnp.sum(jnp.where(ei[None, 0:2] == experts, pad_start[:, None, None], 0), axis=0)
    dest = (start + ei[2:4]).reshape(-1)
    n_blocks = -(-2 * n // SLOT_BLOCK) + N_EXPERTS
    blk_start = jnp.arange(n_blocks, dtype=I32) * SLOT_BLOCK
    block_e = jnp.minimum(jnp.sum((pad_end[None, :] <= blk_start[:, None]).astype(I32), axis=1), N_EXPERTS - 1)
    n_used = (pad_end[-1:] // SLOT_BLOCK).astype(I32)
    xs = _dispatch(dest, x1, n_blocks * SLOT_BLOCK, tm, xs_init)
    yb = _experts(block_e, n_used, xs, lw["w_gate"], lw["w_up"], lw["w_down"], lw["layer"])
    tc = _tile(n, 256)
    return _combine(dest, x1, p, wt.T, lw["w_ple_gate"], lw["w_ple"], lw["ln2_g"], lw["ln2_b"], yb, tc), xs


def kernel(x_prompt, x_sample, p_prompt, p_sample, cache_k, cache_v, cache_idx, state_lru_h, state_lru_conv, state_pool, page_table, w_in, w_out, lru_conv_w, lru_conv_b, lru_wa, lru_ba, lru_wi, lru_bi, lru_lambda, pool_w, pool_scale, ln1_g, ln1_b, w_router_group, b_router_group, w_router_expert, b_router_expert, w_exp_gate, w_exp_up, w_exp_down, w_ple, w_ple_gate, ln2_g, ln2_b):
    bp, tp = x_prompt.shape[:2]
    bs, ts = x_sample.shape[:2]
    assert ts == 1, "the sample group decodes one token per sequence"
    depth = w_in.shape[0]
    past = page_table.shape[1] * PAGE_SIZE
    n_p = bp * tp
    tm_p = _tile(tp, 512)
    tq = _tile(tp, DSA_Q_BLOCK)
    tt = _tile(tp, 256)
    y_att_p = slots_p = slots_s = None
    tabs_p = _rope_tables(jnp.arange(tp))
    tabs_s = _rope_tables(jnp.full((bs,), past, I32))
    cache_idx_t = jnp.swapaxes(cache_idx, 2, 3)

    xp = x_prompt.reshape(n_p, D_MODEL)
    xs = x_sample.reshape(bs, D_MODEL)
    outs = [[] for _ in range(12)]
    row2 = lambda a: a.reshape(1, -1)
    for i in range(depth):
        w_attn, w_mix, wgz = _split_w_in(w_in[i])
        mix_w = w_mix + (lru_conv_w[i], row2(lru_conv_b[i]), lru_wa[i].astype(BF16), row2(lru_ba[i]),
                         lru_wi[i].astype(BF16), row2(lru_bi[i]), row2(lru_lambda[i]),
                         pool_w[i].astype(BF16), row2(pool_scale[i]))
        wout = w_out[i].astype(BF16)
        g1, b1 = row2(ln1_g[i]), row2(ln1_b[i])
        w_rt = jnp.concatenate([w_router_expert[i].T, w_router_group[i].T,
                                jnp.zeros((ROUTER_ROWS - N_EXPERTS - N_EXPERT_GROUPS, D_MODEL), F32)], 0)
        b_rt = jnp.concatenate([b_router_expert[i], b_router_group[i],
                                jnp.zeros((ROUTER_ROWS - N_EXPERTS - N_EXPERT_GROUPS,), F32)])[:, None]
        lw = dict(w_rt=w_rt, b_rt=b_rt, w_gate=w_exp_gate, w_up=w_exp_up, w_down=w_exp_down, layer=i,
                  w_ple_gate=w_ple_gate[i].astype(BF16),
                  w_ple=w_ple[i].astype(BF16), ln2_g=row2(ln2_g[i]), ln2_b=row2(ln2_b[i]))

        q, k, v, kb, vb, iq, iw, ik = _proj_attn(xp, w_attn, tabs_p, tm_p, tp // tm_p)
        y_att_p = _dsa_prompt(q, kb, vb, iq, iw, ik, bp, tp, tq, y_att_p)
        y_lru, y_pool, h_new, conv_new, pool_new = _mix_prompt(xp, mix_w, bp, tp, tt)
        x1 = _merge(xp, y_att_p, y_lru, y_pool, wgz, wout, g1, b1, _tile(n_p, 256))
        xp, slots_p = _ffn(x1, p_prompt[i].reshape(n_p, PLE_DIM), lw, slots_p)
        outs[0].append(k.reshape(bp, tp, N_KV_HEADS, HEAD_DIM))
        outs[1].append(v.reshape(bp, tp, N_KV_HEADS, HEAD_DIM))
        outs[2].append(ik[:, :IDX_DIM].reshape(bp, tp, IDX_DIM))
        outs[3].append(h_new[:, 0])
        outs[4].append(conv_new[:, SUBLANES - (CONV_WIDTH - 1):])
        outs[5].append(pool_new[:, 2 * SUBLANES - POOL_BUF:])

        q, k, v, _, _, iq, iw, ik = _proj_attn(xs, w_attn, tabs_s, bs, 1)
        sel_idx = _dsa_sample_select(page_table, iq, iw, ik[:, :IDX_DIM], cache_idx_t, i)
        y_att = _dsa_sample_attend(sel_idx[:, :, 0], page_table, q, k, v, cache_k, cache_v, i)
        y_lru, y_pool, h_new, xr, xq = _mix_sample(xs, mix_w, state_lru_conv[i], state_lru_h[i], state_pool[i], past)
        x1 = _merge(xs, y_att, y_lru, y_pool, wgz, wout, g1, b1, bs)
        xs, slots_s = _ffn(x1, p_sample[i].reshape(bs, PLE_DIM), lw, slots_s)
        outs[6].append(k.reshape(bs, ts, N_KV_HEADS, HEAD_DIM))
        outs[7].append(v.reshape(bs, ts, N_KV_HEADS, HEAD_DIM))
        outs[8].append(ik[:, :IDX_DIM].reshape(bs, ts, IDX_DIM))
        outs[9].append(h_new)
        outs[10].append(jnp.concatenate([state_lru_conv[i][:, 1:], xr[:, None]], 1))
        outs[11].append(jnp.concatenate([state_pool[i][:, 1:], xq[:, None]], 1))

    return (xp.reshape(bp, tp, D_MODEL), xs.reshape(bs, ts, D_MODEL)) + tuple(jnp.stack(o) for o in outs)
```

```python
import functools

import jax
import jax.numpy as jnp
import numpy as np
from jax import lax
from jax.experimental import pallas as pl
from jax.experimental.pallas import tpu as pltpu

F32 = jnp.float32
BF16 = jnp.bfloat16
I32 = jnp.int32

D_MODEL = 1024
N_HEADS = 8
HEAD_DIM = 128
N_KV_HEADS = 4
KV_GROUP = N_HEADS // N_KV_HEADS
N_IDX_HEADS = 8
IDX_DIM = 64
INDEX_SCALE = (IDX_DIM * N_IDX_HEADS) ** -0.5
TOPK_MAX = 256
ROPE_THETA = 10000.0
PAGE_SIZE = 128
N_LRU_BLOCKS = 8
LRU_BLOCK = D_MODEL // N_LRU_BLOCKS
CONV_WIDTH = 4
LRU_C = 8.0
POOL_WINDOWS = (2, 4, 8, 16)
POOL_GROUP = D_MODEL // len(POOL_WINDOWS)
POOL_BUF = max(POOL_WINDOWS) - 1
N_EXPERT_GROUPS = 4
EXPERTS_PER_GROUP = 8
N_EXPERTS = N_EXPERT_GROUPS * EXPERTS_PER_GROUP
D_EXPERT = 512
PLE_DIM = 256
LN_EPS = 1e-5
DEPTH = 2
DEEPNORM_ALPHA = (2 * DEPTH) ** 0.25
ATTN_SCALE = HEAD_DIM ** -0.5

LANES = 128
SUBLANES = 8
SLOT_BLOCK = 256
DSA_EXTENTS = 8
DSA_Q_BLOCK = 256
RADIX4_MAX_ELEMS = 256 * 1024
SEQ_GROUP = 8
MIX_SEQS = 8
MIX_STEPS = 64
DMA_UNROLL = 8
VMEM_LIMIT = 48 * 1024 * 1024
INT_MIN = -2 ** 31
KEY_NEG_INF = INT_MIN + 0x7FFFFF

_NT = (((1,), (1,)), ((), ()))


def _dot(a, b):
    return jnp.dot(a, b, preferred_element_type=F32)


def _dot_nt(a, b):
    return lax.dot_general(a, b, _NT, preferred_element_type=F32)


def _sigmoid(x):
    return 1.0 / (1.0 + jnp.exp(-x))


def _gelu_tanh(x):
    c = np.float32(np.sqrt(2.0 / np.pi))
    return x * (0.5 * (1.0 + jnp.tanh(c * (x + 0.044715 * (x * x * x)))))


def _softplus(x):
    return jnp.maximum(x, 0.0) + jnp.log1p(jnp.exp(-jnp.abs(x)))


def _layer_norm(r, g, b):
    mu = jnp.mean(r, axis=-1, keepdims=True)
    c = r - mu
    var = jnp.mean(c * c, axis=-1, keepdims=True)
    return c * lax.rsqrt(var + LN_EPS) * g + b


def _sort_key(x):
    bits = pltpu.bitcast(jnp.where(x == 0.0, 0.0, x), I32)
    return jnp.where(bits >= 0, bits, bits ^ 0x7FFFFFFF)


def _params(*sem):
    return pltpu.CompilerParams(dimension_semantics=sem, vmem_limit_bytes=VMEM_LIMIT)


def _full(shape):
    n = len(shape)
    return pl.BlockSpec(shape, lambda *_: (0,) * n)


def _proj_attn_kernel(x_ref, wq, wk, wv, wiq, wiw, wik, cos_ref, sin_ref, cosi_ref, sini_ref,
                      q_o, k_o, v_o, kb_o, vb_o, iq_o, iw_o, ik_o):
    xb = x_ref[...].astype(BF16)
    cos, sin = cos_ref[...], sin_ref[...]
    cosi, sini = cosi_ref[...], sini_ref[...]
    lane = lax.broadcasted_iota(I32, cos.shape, 1)
    low_half = (lane & (IDX_DIM - 1)) < (IDX_DIM // 2)

    def rope128(z):
        return z * cos + pltpu.roll(z, HEAD_DIM // 2, 1) * sin

    def rope64(z):
        partner = jnp.where(low_half, pltpu.roll(z, LANES - IDX_DIM // 2, 1), pltpu.roll(z, IDX_DIM // 2, 1))
        return z * cosi + partner * sini

    q = _dot(xb, wq[...])
    for h in range(N_HEADS):
        sl = slice(h * LANES, (h + 1) * LANES)
        q_o[:, sl] = (rope128(q[:, sl]) * ATTN_SCALE).astype(BF16)
    k = _dot(xb, wk[...])
    for c in range(N_KV_HEADS):
        sl = slice(c * LANES, (c + 1) * LANES)
        kr = rope128(k[:, sl])
        k_o[:, sl] = kr
        kb_o[:, sl] = kr.astype(BF16)
    v = _dot(xb, wv[...])
    v_o[...] = v
    vb_o[...] = v.astype(BF16)
    iq = _dot(xb, wiq[...])
    for j in range(N_IDX_HEADS * IDX_DIM // LANES):
        sl = slice(j * LANES, (j + 1) * LANES)
        iq_o[:, sl] = rope64(iq[:, sl])
    iw_o[...] = _dot(xb, wiw[...])
    ik_o[...] = rope64(_dot(xb, wik[...]))


def _proj_attn(x, w, tabs, tm, n_tab_blocks):
    n = x.shape[0]
    cos, sin, cosi, sini = tabs
    row = lambda i: (i, 0)
    tab = lambda i: (i % n_tab_blocks, 0)
    wspec = lambda a: _full(a.shape)
    outs = [
        ((n, 1024), BF16), ((n, 512), F32), ((n, 512), F32), ((n, 512), BF16), ((n, 512), BF16),
        ((n, 512), F32), ((n, LANES), F32), ((n, LANES), F32)]
    return pl.pallas_call(
        _proj_attn_kernel,
        grid=(n // tm,),
        in_specs=[pl.BlockSpec((tm, D_MODEL), row)] + [wspec(a) for a in w]
        + [pl.BlockSpec((tm, LANES), tab)] * 4,
        out_specs=[pl.BlockSpec((tm, s[1]), row) for s, _ in outs],
        out_shape=[jax.ShapeDtypeStruct(s, d) for s, d in outs],
        compiler_params=_params("parallel"),
        name="proj_attn",
    )(x, *w, cos, sin, cosi, sini)


def _count_rows(mask):
    return jnp.sum(jnp.where(mask, 1.0, 0.0), axis=1, keepdims=True)


def _dsa_prompt_kernel(q_ref, kb_ref, vb_ref, iq_ref, iw_ref, ik_ref, *rest, tq, t_len, q0, topk, idx_bits):
    o_ref, p_sc = rest[-2:]
    qi = q0 + pl.program_id(1)
    ikb = ik_ref[...].astype(BF16)
    iq = iq_ref[...]
    iw = iw_ref[...] * INDEX_SCALE
    lane = lax.broadcasted_iota(I32, (tq, LANES), 1)
    score = jnp.zeros((tq, t_len), F32)
    for h in range(N_IDX_HEADS):
        chunk = iq[:, (h // 2) * LANES:(h // 2 + 1) * LANES]
        keep = (lane < IDX_DIM) if h % 2 == 0 else (lane >= IDX_DIM)
        s = _dot_nt(jnp.where(keep, chunk, 0.0).astype(BF16), ikb)
        score = score + jnp.maximum(s, 0.0) * iw[:, h:h + 1]
    qpos = qi * tq + lax.broadcasted_iota(I32, (tq, 1), 0)
    kpos = lax.broadcasted_iota(I32, (tq, t_len), 1)
    causal = kpos <= qpos
    key = _sort_key(jnp.where(causal, score, -jnp.inf))

    if tq * t_len <= RADIX4_MAX_ELEMS:
        def thr_body(b, t):
            step = lax.shift_left(jnp.int32(1), 30 - 2 * b)
            cands = [t + step, t + 2 * step, t + 3 * step]
            ok = [_count_rows(key >= c) >= topk for c in cands]
            return jnp.where(ok[2], cands[2], jnp.where(ok[1], cands[1], jnp.where(ok[0], cands[0], t)))
        n_steps = 16
    else:
        def thr_body(b, t):
            cand = t + lax.shift_left(jnp.int32(1), 31 - b)
            return jnp.where(_count_rows(key >= cand) >= topk, cand, t)
        n_steps = 32

    thr = lax.fori_loop(0, n_steps, thr_body, jnp.full((tq, 1), INT_MIN, I32))
    gt = key > thr
    eq = key == thr
    need = topk - _count_rows(gt)
    tie = (_count_rows(eq) > need) & (thr > KEY_NEG_INF)
    p_sc[...] = jnp.full((tq, 1), t_len, I32)

    @pl.when(jnp.max(jnp.where(tie, 1.0, 0.0)) > 0.0)
    def _():
        def pos_body(b, p):
            cand = p + lax.shift_left(jnp.int32(1), idx_bits - 1 - b)
            return jnp.where(_count_rows(eq & (kpos < cand)) < need, cand, p)
        p_sc[...] = lax.fori_loop(0, idx_bits, pos_body, jnp.zeros((tq, 1), I32))

    sel = causal & (gt | (eq & (kpos <= p_sc[...])))
    bias = jnp.where(sel, 0.0, -jnp.inf)
    for c in range(N_KV_HEADS):
        kc = kb_ref[:, c * LANES:(c + 1) * LANES]
        vc = vb_ref[:, c * LANES:(c + 1) * LANES]
        vx = jnp.concatenate([vc, jnp.ones_like(vc)], axis=1)
        for g in range(KV_GROUP):
            h = c * KV_GROUP + g
            s = _dot_nt(q_ref[:, h * LANES:(h + 1) * LANES], kc) + bias
            m = jnp.max(s, axis=1, keepdims=True)
            pv = _dot(jnp.exp(s - m).astype(BF16), vx)
            o_ref[:, h * LANES:(h + 1) * LANES] = pv[:, 0:LANES] / pv[:, LANES:LANES + 1]


def _dsa_prompt(q, kb, vb, iq, iw, ik, batch, t_len, tq, y_init=None):
    topk = min(TOPK_MAX, t_len // 4)
    nq = t_len // tq
    n_var = max(d for d in range(1, DSA_EXTENTS + 1) if nq % d == 0)
    per = nq // n_var
    kb, vb, ik = (a.reshape(batch, t_len, a.shape[-1]) for a in (kb, vb, ik))
    y = jnp.zeros((batch * t_len, 1024), F32) if y_init is None else y_init
    for v in range(n_var):
        ext = (v + 1) * per * tq
        qrow = lambda b, i, v=v: (b * nq + v * per + i, 0)
        brow = lambda b, i: (b, 0, 0)
        in_specs = [pl.BlockSpec((tq, 1024), qrow), pl.BlockSpec((None, ext, 512), brow),
                    pl.BlockSpec((None, ext, 512), brow), pl.BlockSpec((tq, 512), qrow),
                    pl.BlockSpec((tq, LANES), qrow), pl.BlockSpec((None, ext, LANES), brow)]
        in_specs.append(pl.BlockSpec(memory_space=pl.ANY))
        y = pl.pallas_call(
            functools.partial(_dsa_prompt_kernel, tq=tq, t_len=ext, q0=v * per, topk=topk,
                              idx_bits=max(1, int(np.ceil(np.log2(ext))))),
            grid=(batch, per),
            in_specs=in_specs,
            out_specs=pl.BlockSpec((tq, 1024), qrow),
            out_shape=jax.ShapeDtypeStruct((batch * t_len, 1024), F32),
            scratch_shapes=[pltpu.VMEM((tq, 1), I32)],
            input_output_aliases={6: 0},
            compiler_params=_params("parallel", "arbitrary"),
            name="dsa_prompt",
        )(q, kb, vb, iq, iw, ik, y)
    return y


def _dsa_sample_select_kernel(pt_ref, iq_ref, iw_ref, ikn_ref, cidx_ref, idx_o,
                              ikbuf, s_sc, c_sc, sem, *, layer, n_pages, topk, idx_bits, group):
    g0 = pl.program_id(0) * group
    past = n_pages * PAGE_SIZE

    def page_copy(s, j):
        slot = s % 2
        return pltpu.make_async_copy(cidx_ref.at[layer, pt_ref[(g0 + s) * n_pages + j]],
                                     ikbuf.at[slot, j], sem.at[slot])

    def issue_seq(s):
        def body(j, c):
            page_copy(s, j).start()
            return c
        lax.fori_loop(0, n_pages, body, 0, unroll=DMA_UNROLL)

    def drain_seq(s):
        def body(j, c):
            page_copy(s, j).wait()
            return c
        lax.fori_loop(0, n_pages, body, 0, unroll=DMA_UNROLL)

    issue_seq(0)

    def seq_body(s, c):
        @pl.when(s + 1 < group)
        def _():
            issue_seq(s + 1)
        drain_seq(s)
        slot = s % 2
        iq8 = iq_ref[s].astype(BF16)
        w8 = iw_ref[s] * INDEX_SCALE
        for j in range(n_pages):
            s8 = _dot(iq8, ikbuf[slot, j].astype(BF16))
            s_sc[s, j:j + 1, :] = jnp.sum(jnp.maximum(s8, 0.0) * w8, axis=0, keepdims=True)
        return c

    lax.fori_loop(0, group, seq_body, 0)

    w_all = iw_ref[...] * INDEX_SCALE
    own = jnp.sum(iq_ref[...].astype(BF16).astype(F32) * ikn_ref[...].astype(BF16).astype(F32),
                  axis=2, keepdims=True)
    own = jnp.sum(jnp.maximum(own, 0.0) * w_all, axis=1, keepdims=True)

    key = _sort_key(s_sc[...])
    key_own = _sort_key(own)
    pos = (lax.broadcasted_iota(I32, key.shape, 1) * PAGE_SIZE + lax.broadcasted_iota(I32, key.shape, 2))

    def count(mask, mask_own):
        c = jnp.sum(jnp.where(mask, 1.0, 0.0), axis=1, keepdims=True)
        return jnp.sum(c, axis=2, keepdims=True) + jnp.where(mask_own, 1.0, 0.0)

    def thr_body(i, t):
        cand = t + lax.shift_left(jnp.int32(1), 31 - i)
        return jnp.where(count(key >= cand, key_own >= cand) >= topk, cand, t)

    thr = lax.fori_loop(0, 32, thr_body, jnp.full((group, 1, 1), INT_MIN, I32))
    gt, eq = key > thr, key == thr
    gt_own, eq_own = key_own > thr, key_own == thr
    need = topk - count(gt, gt_own)

    def pos_body(i, p):
        cand = p + lax.shift_left(jnp.int32(1), idx_bits - 1 - i)
        return jnp.where(count(eq & (pos < cand), eq_own & (past < cand)) < need, cand, p)

    plast = lax.fori_loop(0, idx_bits, pos_body, jnp.zeros((group, 1, 1), I32))
    s_sc[...] = jnp.where(gt | (eq & (pos <= plast)), 1.0, 0.0)

    r_i = lax.broadcasted_iota(I32, (PAGE_SIZE, PAGE_SIZE), 0)
    c_i = lax.broadcasted_iota(I32, (PAGE_SIZE, PAGE_SIZE), 1)
    tri_incl = jnp.where(r_i <= c_i, 1.0, 0.0).astype(BF16)
    pr = lax.broadcasted_iota(I32, (n_pages, n_pages), 0)
    pc = lax.broadcasted_iota(I32, (n_pages, n_pages), 1)
    tri_pages = jnp.where(pc < pr, 1.0, 0.0).astype(BF16)
    rank = lax.broadcasted_iota(I32, (topk, PAGE_SIZE), 0).astype(F32)

    def compact(s, c):
        within = _dot(s_sc[s].astype(BF16), tri_incl)
        tot = jnp.broadcast_to(within[:, PAGE_SIZE - 1:PAGE_SIZE], (n_pages, PAGE_SIZE)).astype(BF16)
        c_sc[...] = within + _dot(tri_pages, tot)

        def acc_body(j, acc):
            return acc + jnp.where(c_sc[pl.ds(j, 1), :] <= rank, 1.0, 0.0)

        acc = lax.fori_loop(0, n_pages, acc_body, jnp.zeros((topk, PAGE_SIZE), F32))
        idx_o[s] = jnp.sum(acc, axis=1, keepdims=True).astype(I32)
        return c

    lax.fori_loop(0, group, compact, 0)


def _dsa_sample_select(page_table, iq, iw, ik_new, cache_idx_t, layer):
    bd, n_pages = page_table.shape
    past = n_pages * PAGE_SIZE
    topk = min(TOPK_MAX, (past + 1) // 4)
    idx_bits = int(np.floor(np.log2(past))) + 1
    group = SEQ_GROUP if bd % SEQ_GROUP == 0 else bd
    grp = lambda g, pt: (g, 0, 0)
    grid_spec = pltpu.PrefetchScalarGridSpec(
        num_scalar_prefetch=1,
        grid=(bd // group,),
        in_specs=[pl.BlockSpec((group, N_IDX_HEADS, IDX_DIM), grp),
                  pl.BlockSpec((group, N_IDX_HEADS, 1), grp),
                  pl.BlockSpec((group, 1, IDX_DIM), grp),
                  pl.BlockSpec(memory_space=pl.ANY)],
        out_specs=pl.BlockSpec((group, topk, 1), grp),
        scratch_shapes=[pltpu.VMEM((2, n_pages, IDX_DIM, PAGE_SIZE), F32), pltpu.VMEM((group, n_pages, PAGE_SIZE), F32),
                        pltpu.VMEM((n_pages, PAGE_SIZE), F32), pltpu.SemaphoreType.DMA((2,))])
    return pl.pallas_call(
        functools.partial(_dsa_sample_select_kernel, layer=layer, n_pages=n_pages, topk=topk, idx_bits=idx_bits,
                          group=group),
        grid_spec=grid_spec,
        out_shape=jax.ShapeDtypeStruct((bd, topk, 1), I32),
        compiler_params=_params("arbitrary"),
        name="dsa_sample_select",
    )(page_table.reshape(-1), iq.reshape(bd, N_IDX_HEADS, IDX_DIM), iw[:, :N_IDX_HEADS, None],
      ik_new[:, None, :], cache_idx_t)


def _dsa_sample_attend_kernel(idx_ref, pt_ref, q_ref, kn_ref, vn_ref, idxv_ref, ck_ref, cv_ref, o_ref,
                              kbuf, vbuf, sem, *, layer, n_pages, topk):
    b = pl.program_id(0)
    past = n_pages * PAGE_SIZE

    def row_copies(r):
        pidx = jnp.minimum(idx_ref[b * topk + r], past - 1)
        phys = pt_ref[b * n_pages + pidx // PAGE_SIZE]
        off = pidx % PAGE_SIZE
        return (pltpu.make_async_copy(ck_ref.at[layer, phys, off], kbuf.at[r], sem.at[0]),
                pltpu.make_async_copy(cv_ref.at[layer, phys, off], vbuf.at[r], sem.at[1]))

    def issue(r, c):
        for cp in row_copies(r):
            cp.start()
        return c

    def drain(r, c):
        for cp in row_copies(r):
            cp.wait()
        return c

    lax.fori_loop(0, topk, issue, 0, unroll=DMA_UNROLL)
    lax.fori_loop(0, topk, drain, 0, unroll=DMA_UNROLL)

    own = idxv_ref[...] >= past
    k_sel = jnp.where(own, kn_ref[...][None], kbuf[...])
    v_sel = jnp.where(own, vn_ref[...][None], vbuf[...])
    for g in range(KV_GROUP):
        s = jnp.sum(k_sel * q_ref[g][None], axis=-1, keepdims=True)
        m = jnp.max(s, axis=0, keepdims=True)
        p = jnp.exp(s - m)
        l = jnp.sum(p, axis=0)
        o_ref[g] = jnp.sum(p * v_sel, axis=0) / l


def _dsa_sample_attend(idx, page_table, q, k_new, v_new, cache_k, cache_v, layer):
    bd, n_pages = page_table.shape
    topk = idx.shape[1]
    qg = q.astype(F32).reshape(bd, N_KV_HEADS, KV_GROUP, HEAD_DIM).transpose(0, 2, 1, 3)
    head = lambda b, *_: (b, 0, 0)
    grid_spec = pltpu.PrefetchScalarGridSpec(
        num_scalar_prefetch=2,
        grid=(bd,),
        in_specs=[pl.BlockSpec((None, KV_GROUP, N_KV_HEADS, HEAD_DIM), lambda b, *_: (b, 0, 0, 0)),
                  pl.BlockSpec((None, N_KV_HEADS, HEAD_DIM), head),
                  pl.BlockSpec((None, N_KV_HEADS, HEAD_DIM), head),
                  pl.BlockSpec((None, topk, 1, 1), lambda b, *_: (b, 0, 0, 0)),
                  pl.BlockSpec(memory_space=pl.ANY), pl.BlockSpec(memory_space=pl.ANY)],
        out_specs=pl.BlockSpec((None, KV_GROUP, N_KV_HEADS, HEAD_DIM), lambda b, *_: (b, 0, 0, 0)),
        scratch_shapes=[pltpu.VMEM((topk, N_KV_HEADS, HEAD_DIM), F32), pltpu.VMEM((topk, N_KV_HEADS, HEAD_DIM), F32),
                        pltpu.SemaphoreType.DMA((2,))])
    o = pl.pallas_call(
        functools.partial(_dsa_sample_attend_kernel, layer=layer, n_pages=n_pages, topk=topk),
        grid_spec=grid_spec,
        out_shape=jax.ShapeDtypeStruct((bd, KV_GROUP, N_KV_HEADS, HEAD_DIM), F32),
        compiler_params=_params("arbitrary"),
        name="dsa_sample_attend",
    )(idx.reshape(-1), page_table.reshape(-1), qg, k_new.reshape(bd, N_KV_HEADS, HEAD_DIM),
      v_new.reshape(bd, N_KV_HEADS, HEAD_DIM), idx.reshape(bd, topk, 1, 1), cache_k, cache_v)
    return o.transpose(0, 2, 1, 3).reshape(bd, N_HEADS * HEAD_DIM)


def _lru_gates(xc, wa, ba, wi, bi, lam):
    xcb = xc.astype(BF16)
    a_parts, u_parts = [], []
    for n in range(N_LRU_BLOCKS):
        sl = slice(n * LRU_BLOCK, (n + 1) * LRU_BLOCK)
        r = _sigmoid(_dot(xcb[:, sl], wa[n]) + ba[:, sl])
        i = _sigmoid(_dot(xcb[:, sl], wi[n]) + bi[:, sl])
        log_a = -LRU_C * r * _softplus(-lam[:, sl])
        a_parts.append(jnp.exp(log_a))
        th = jnp.tanh(log_a)
        u_parts.append(jnp.sqrt(-2.0 * th / (1.0 - th)) * i * xc[:, sl])
    return a_parts, u_parts


def _pool_mix(window_sum, xq, cnt, pool_w, pool_scale, g):
    sl = slice(g * POOL_GROUP, (g + 1) * POOL_GROUP)
    pooled = window_sum / cnt - xq[:, sl]
    return _dot(pooled.astype(BF16), pool_w[g]) * pool_scale[:, sl]


def _mix_prompt_kernel(x_ref, wxr, wgr, wxq, convw, convb, wa, ba, wi, bi, lam, poolw, pscale,
                       ylru_o, ypool_o, h_o, conv_o, pool_o,
                       xr_ext, xq_ext, a_sc, u_sc, h_sc, h_carry, *, nb, tt):
    t = pl.program_id(1)
    halo_r, halo_q = SUBLANES, 2 * SUBLANES
    rows = nb * tt

    @pl.when(t == 0)
    def _():
        xr_ext[:, 0:halo_r] = jnp.zeros((nb, halo_r, D_MODEL), F32)
        xq_ext[:, 0:halo_q] = jnp.zeros((nb, halo_q, D_MODEL), F32)
        h_carry[...] = jnp.zeros_like(h_carry)

    @pl.when(t > 0)
    def _():
        xr_ext[:, 0:halo_r] = xr_ext[:, tt:tt + halo_r]
        xq_ext[:, 0:halo_q] = xq_ext[:, tt:tt + halo_q]

    xb = x_ref[...].reshape(rows, D_MODEL).astype(BF16)
    xr3 = _dot(xb, wxr[...]).reshape(nb, tt, D_MODEL)
    xq3 = _dot(xb, wxq[...]).reshape(nb, tt, D_MODEL)
    xr_ext[:, halo_r:halo_r + tt] = xr3
    xq_ext[:, halo_q:halo_q + tt] = xq3

    cw = convw[...]
    xc = convb[...] + cw[CONV_WIDTH - 1:CONV_WIDTH] * xr3
    for j in range(CONV_WIDTH - 1):
        o = halo_r - (CONV_WIDTH - 1) + j
        xc = xc + cw[j:j + 1] * xr_ext[:, o:o + tt]
    a_parts, u_parts = _lru_gates(xc.reshape(rows, D_MODEL), wa, ba[...], wi, bi[...], lam[...])
    for n in range(N_LRU_BLOCKS):
        sl = slice(n * LRU_BLOCK, (n + 1) * LRU_BLOCK)
        a_sc[:, sl] = a_parts[n]
        u_sc[:, sl] = u_parts[n]

    a3 = pltpu.einshape("btd->tbd", a_sc[...].reshape(nb, tt, D_MODEL))
    u3 = pltpu.einshape("btd->tbd", u_sc[...].reshape(nb, tt, D_MODEL))
    h = h_carry[...]
    for s in range(tt):
        h = a3[s] * h + u3[s]
        h_sc[s] = h
    h_carry[...] = h
    hs = pltpu.einshape("tbd->btd", h_sc[...]).reshape(rows, D_MODEL)
    ylru_o[...] = (hs * _gelu_tanh(_dot(xb, wgr[...]))).reshape(nb, tt, D_MODEL)

    posn = t * tt + lax.broadcasted_iota(I32, (1, tt, 1), 1)
    for g, w in enumerate(POOL_WINDOWS):
        sl = slice(g * POOL_GROUP, (g + 1) * POOL_GROUP)
        acc = xq_ext[:, :, sl]
        d = 1
        while d < w:
            acc = acc + pltpu.roll(acc, d, 1)
            d *= 2
        cnt = jnp.minimum(posn + 1, w).astype(F32)
        pooled = (acc[:, halo_q:] / cnt - xq3[:, :, sl]).reshape(rows, POOL_GROUP)
        mixed = _dot(pooled.astype(BF16), poolw[g]) * pscale[:, sl]
        ypool_o[:, :, sl] = mixed.reshape(nb, tt, POOL_GROUP)

    @pl.when(t == pl.num_programs(1) - 1)
    def _():
        h_o[...] = h
        conv_o[...] = xr_ext[:, tt:tt + halo_r]
        pool_o[...] = xq_ext[:, tt:tt + halo_q]


def _mix_prompt(x, w, batch, t_len):
    nb = _tile(batch, MIX_SEQS)
    tt = _tile(t_len, MIX_STEPS)
    blk = pl.BlockSpec((nb, tt, D_MODEL), lambda b, t: (b, t, 0))
    per_b = lambda b, t: (b, 0, 0)
    outs = [((batch, t_len, D_MODEL), F32), ((batch, t_len, D_MODEL), F32),
            ((batch, D_MODEL), F32), ((batch, SUBLANES, D_MODEL), F32), ((batch, 2 * SUBLANES, D_MODEL), F32)]
    return pl.pallas_call(
        functools.partial(_mix_prompt_kernel, nb=nb, tt=tt),
        grid=(batch // nb, t_len // tt),
        in_specs=[blk] + [_full(a.shape) for a in w],
        out_specs=[blk, blk, pl.BlockSpec((nb, D_MODEL), lambda b, t: (b, 0)),
                   pl.BlockSpec((nb, SUBLANES, D_MODEL), per_b), pl.BlockSpec((nb, 2 * SUBLANES, D_MODEL), per_b)],
        out_shape=[jax.ShapeDtypeStruct(s, d) for s, d in outs],
        scratch_shapes=[pltpu.VMEM((nb, tt + SUBLANES, D_MODEL), F32), pltpu.VMEM((nb, tt + 2 * SUBLANES, D_MODEL), F32),
                        pltpu.VMEM((nb * tt, D_MODEL), F32), pltpu.VMEM((nb * tt, D_MODEL), F32),
                        pltpu.VMEM((tt, nb, D_MODEL), F32), pltpu.VMEM((nb, D_MODEL), F32)],
        compiler_params=_params("parallel", "arbitrary"),
        name="mix_prompt",
    )(x, *w)


def _mix_sample_kernel(x_ref, wxr, wgr, wxq, convw, convb, wa, ba, wi, bi, lam, poolw, pscale,
                       conv_ref, h_ref, pool_ref, ylru_o, ypool_o, h_o, xr_o, xq_o, *, cnt_pos):
    xb = x_ref[...].astype(BF16)
    xr = _dot(xb, wxr[...])
    xq = _dot(xb, wxq[...])
    cw = convw[...]
    xc = convb[...] + cw[CONV_WIDTH - 1:CONV_WIDTH] * xr
    for j in range(CONV_WIDTH - 1):
        xc = xc + cw[j:j + 1] * conv_ref[j]
    a_parts, u_parts = _lru_gates(xc, wa, ba[...], wi, bi[...], lam[...])
    gate = _gelu_tanh(_dot(xb, wgr[...]))
    for n in range(N_LRU_BLOCKS):
        sl = slice(n * LRU_BLOCK, (n + 1) * LRU_BLOCK)
        h = a_parts[n] * h_ref[:, sl] + u_parts[n]
        h_o[:, sl] = h
        ylru_o[:, sl] = h * gate[:, sl]
    for g, w in enumerate(POOL_WINDOWS):
        sl = slice(g * POOL_GROUP, (g + 1) * POOL_GROUP)
        acc = xq[:, sl]
        for j in range(1, w):
            acc = acc + pool_ref[POOL_BUF - j, :, sl]
        ypool_o[:, sl] = _pool_mix(acc, xq, float(min(cnt_pos, w)), poolw, pscale[...], g)
    xr_o[...] = xr
    xq_o[...] = xq


def _mix_sample(x, w, conv_state, h_state, pool_state, past):
    bd = x.shape[0]
    args = (x, *w, conv_state.transpose(1, 0, 2), h_state, pool_state.transpose(1, 0, 2))
    shp = jax.ShapeDtypeStruct((bd, D_MODEL), F32)
    return pl.pallas_call(
        functools.partial(_mix_sample_kernel, cnt_pos=past + 1),
        in_specs=[_full(a.shape) for a in args],
        out_specs=[_full((bd, D_MODEL))] * 5,
        out_shape=[shp] * 5,
        grid=(1,),
        compiler_params=_params("arbitrary"),
        name="mix_sample",
    )(*args)


def _merge_kernel(x_ref, ya_ref, yl_ref, yp_ref, wgz, wout, g_ref, b_ref, x1_o):
    x = x_ref[...]
    gz = _dot(x.astype(BF16), wgz[...])
    merged = (_sigmoid(gz[:, 0:D_MODEL]) * ya_ref[...] + _sigmoid(gz[:, D_MODEL:2 * D_MODEL]) * yl_ref[...]
              + _sigmoid(gz[:, 2 * D_MODEL:3 * D_MODEL]) * yp_ref[...])
    r = DEEPNORM_ALPHA * x + _dot(merged.astype(BF16), wout[...])
    x1_o[...] = _layer_norm(r, g_ref[...], b_ref[...])


def _merge(x, ya, yl, yp, wgz, wout, g, b, tm):
    n = x.shape[0]
    row = pl.BlockSpec((tm, D_MODEL), lambda i: (i, 0))
    return pl.pallas_call(
        _merge_kernel,
        grid=(n // tm,),
        in_specs=[row] * 4 + [_full(wgz.shape), _full(wout.shape), _full(g.shape), _full(b.shape)],
        out_specs=row,
        out_shape=jax.ShapeDtypeStruct((n, D_MODEL), F32),
        compiler_params=_params("parallel"),
        name="merge",
    )(x, ya, yl, yp, wgz, wout, g, b)


ROUTER_ROWS = 40


def _first_argmax(v, n):
    m = jnp.max(v, axis=0, keepdims=True)
    rows = lax.broadcasted_iota(I32, v.shape, 0)
    return m, jnp.min(jnp.where(v == m, rows, n), axis=0, keepdims=True)


def _router_kernel(x_ref, w_ref, b_ref, ei_o, wt_o, cnt_o, carry, *, tm):
    i = pl.program_id(0)

    @pl.when(i == 0)
    def _():
        carry[...] = jnp.zeros_like(carry)

    def split(v):
        hi = v.astype(BF16)
        return hi, (v - hi.astype(F32)).astype(BF16)

    x_hi, x_lo = split(x_ref[...])
    w_hi, w_lo = split(w_ref[...])
    logits = _dot_nt(w_hi, x_hi) + (_dot_nt(w_hi, x_lo) + _dot_nt(w_lo, x_hi)) + b_ref[...]
    le = logits[0:N_EXPERTS]
    lg = logits[N_EXPERTS:N_EXPERTS + N_EXPERT_GROUPS]

    gmax, gidx = _first_argmax(lg, N_EXPERT_GROUPS)
    p_top = 1.0 / jnp.sum(jnp.exp(lg - gmax), axis=0, keepdims=True)
    le_g = jnp.zeros((EXPERTS_PER_GROUP, tm), F32)
    for gi in range(N_EXPERT_GROUPS):
        le_g = le_g + jnp.where(gidx == gi, le[gi * EXPERTS_PER_GROUP:(gi + 1) * EXPERTS_PER_GROUP], 0.0)
    m1, i1 = _first_argmax(le_g, EXPERTS_PER_GROUP)
    rows8 = lax.broadcasted_iota(I32, le_g.shape, 0)
    m2, i2 = _first_argmax(jnp.where(rows8 == i1, -jnp.inf, le_g), EXPERTS_PER_GROUP)
    z = jnp.sum(jnp.exp(le_g - m1), axis=0, keepdims=True)
    p1 = 1.0 / z
    p2 = jnp.exp(m2 - m1) / z
    e1 = gidx * EXPERTS_PER_GROUP + i1
    e2 = gidx * EXPERTS_PER_GROUP + i2

    rows = lax.broadcasted_iota(I32, (N_EXPERTS, tm), 0)
    hit1, hit2 = rows == e1, rows == e2
    onehot = jnp.where(hit1 | hit2, 1.0, 0.0)
    r_i = lax.broadcasted_iota(I32, (tm, tm), 0)
    c_i = lax.broadcasted_iota(I32, (tm, tm), 1)
    before = _dot(onehot.astype(BF16), jnp.where(r_i < c_i, 1.0, 0.0).astype(BF16)) + carry[...]
    carry[...] = carry[...] + jnp.sum(onehot, axis=1, keepdims=True)

    ei_o[0:1, :] = e1
    ei_o[1:2, :] = e2
    ei_o[2:3, :] = jnp.sum(jnp.where(hit1, before, 0.0), axis=0, keepdims=True).astype(I32)
    ei_o[3:4, :] = jnp.sum(jnp.where(hit2, before, 0.0), axis=0, keepdims=True).astype(I32)
    ei_o[4:SUBLANES, :] = jnp.zeros((SUBLANES - 4, tm), I32)
    wt_o[0:1, :] = p1 / (p1 + p2) * p_top
    wt_o[1:2, :] = p2 / (p1 + p2) * p_top
    wt_o[2:SUBLANES, :] = jnp.zeros((SUBLANES - 2, tm), F32)
    cnt_o[...] = carry[...]


def _router(x1, w_rt, b_rt, tm):
    n = x1.shape[0]
    col = pl.BlockSpec((SUBLANES, tm), lambda i: (0, i))
    return pl.pallas_call(
        functools.partial(_router_kernel, tm=tm),
        grid=(n // tm,),
        in_specs=[pl.BlockSpec((tm, D_MODEL), lambda i: (i, 0)), _full(w_rt.shape), _full(b_rt.shape)],
        out_specs=[col, col, _full((N_EXPERTS, 1))],
        out_shape=[jax.ShapeDtypeStruct((SUBLANES, n), I32), jax.ShapeDtypeStruct((SUBLANES, n), F32),
                   jax.ShapeDtypeStruct((N_EXPERTS, 1), F32)],
        scratch_shapes=[pltpu.VMEM((N_EXPERTS, 1), F32)],
        compiler_params=_params("arbitrary"),
        name="router",
    )(x1, w_rt, b_rt)


def _dispatch_kernel(dest_ref, x_ref, xs_in, xs_out, sem, *, tm, n):
    del xs_in
    i = pl.program_id(0)

    def row_copy(r, k):
        return pltpu.make_async_copy(x_ref.at[pl.ds(r, 1)], xs_out.at[pl.ds(dest_ref[k * n + i * tm + r], 1)], sem)

    def issue(r, c):
        row_copy(r, 0).start()
        row_copy(r, 1).start()
        return c

    def drain(r, c):
        row_copy(r, 0).wait()
        row_copy(r, 1).wait()
        return c

    lax.fori_loop(0, tm, issue, 0, unroll=DMA_UNROLL)
    lax.fori_loop(0, tm, drain, 0, unroll=DMA_UNROLL)


def _dispatch(dest, x1, n_slots, tm, xs_init=None):
    n = x1.shape[0]
    if xs_init is None:
        xs_init = jnp.zeros((n_slots, D_MODEL), F32)
    grid_spec = pltpu.PrefetchScalarGridSpec(
        num_scalar_prefetch=1,
        grid=(n // tm,),
        in_specs=[pl.BlockSpec((tm, D_MODEL), lambda i, d: (i, 0)), pl.BlockSpec(memory_space=pl.ANY)],
        out_specs=pl.BlockSpec(memory_space=pl.ANY),
        scratch_shapes=[pltpu.SemaphoreType.DMA])
    return pl.pallas_call(
        functools.partial(_dispatch_kernel, tm=tm, n=n),
        grid_spec=grid_spec,
        out_shape=jax.ShapeDtypeStruct((n_slots, D_MODEL), F32),
        input_output_aliases={2: 0},
        compiler_params=_params("arbitrary"),
        name="dispatch",
    )(dest, x1, xs_init)


def _expert_kernel(be_ref, nu_ref, xs_ref, wg_ref, wu_ref, wd_ref, y_o, wg_b, wu_b, wd_b):
    i = pl.program_id(0)

    @pl.when(i < nu_ref[0])
    def _():
        @pl.when((i == 0) | (be_ref[i] != be_ref[jnp.maximum(i - 1, 0)]))
        def _():
            wg_b[...] = wg_ref[...].astype(BF16)
            wu_b[...] = wu_ref[...].astype(BF16)
            wd_b[...] = wd_ref[...].astype(BF16)

        xb = xs_ref[...].astype(BF16)
        gate = _dot(xb, wg_b[...])
        h = gate * _sigmoid(gate) * _dot(xb, wu_b[...])
        y_o[...] = _dot(h.astype(BF16), wd_b[...])

    @pl.when(i >= nu_ref[0])
    def _():
        y_o[...] = jnp.zeros_like(y_o)


def _experts(block_e, n_used, xs, w_gate, w_up, w_down, layer):
    n_blocks = xs.shape[0] // SLOT_BLOCK
    blk = pl.BlockSpec((SLOT_BLOCK, D_MODEL), lambda i, be, nu: (i, 0))
    wsel = lambda i, be, nu: (layer, be[i], 0, 0)
    grid_spec = pltpu.PrefetchScalarGridSpec(
        num_scalar_prefetch=2,
        grid=(n_blocks,),
        in_specs=[blk,
                  pl.BlockSpec((None, None, D_MODEL, D_EXPERT), wsel),
                  pl.BlockSpec((None, None, D_MODEL, D_EXPERT), wsel),
                  pl.BlockSpec((None, None, D_EXPERT, D_MODEL), wsel)],
        out_specs=blk,
        scratch_shapes=[pltpu.VMEM((D_MODEL, D_EXPERT), BF16), pltpu.VMEM((D_MODEL, D_EXPERT), BF16),
                        pltpu.VMEM((D_EXPERT, D_MODEL), BF16)])
    return pl.pallas_call(
        _expert_kernel,
        grid_spec=grid_spec,
        out_shape=jax.ShapeDtypeStruct(xs.shape, F32),
        compiler_params=_params("arbitrary"),
        name="experts",
    )(block_e, n_used, xs, w_gate, w_up, w_down)


def _combine_kernel(dest_ref, x_ref, p_ref, wt_ref, wpg, wple, g_ref, b_ref, yb_ref, x2_o, buf, sem, *, tm, n):
    i = pl.program_id(0)

    def row_copy(r, k):
        return pltpu.make_async_copy(yb_ref.at[pl.ds(dest_ref[k * n + i * tm + r], 1)],
                                     buf.at[k, pl.ds(r, 1)], sem.at[k])

    def issue(r, c):
        row_copy(r, 0).start()
        row_copy(r, 1).start()
        return c

    def drain(r, c):
        row_copy(r, 0).wait()
        row_copy(r, 1).wait()
        return c

    for r in range(tm):
        issue(r, 0)
    x = x_ref[...]
    ple = _sigmoid(_dot(x.astype(BF16), wpg[...])) * _dot(p_ref[...].astype(BF16), wple[...])
    lax.fori_loop(0, tm, drain, 0, unroll=DMA_UNROLL)
    wt = wt_ref[...]
    y = wt[:, 0:1] * buf[0] + wt[:, 1:2] * buf[1]
    x2_o[...] = _layer_norm(DEEPNORM_ALPHA * x + y + ple, g_ref[...], b_ref[...])


def _combine(dest, x1, p, wt, wpg, wple, g, b, yb, tm):
    n = x1.shape[0]
    row = lambda i, d: (i, 0)
    grid_spec = pltpu.PrefetchScalarGridSpec(
        num_scalar_prefetch=1,
        grid=(n // tm,),
        in_specs=[pl.BlockSpec((tm, D_MODEL), row), pl.BlockSpec((tm, PLE_DIM), row),
                  pl.BlockSpec((tm, SUBLANES), row)]
        + [pl.BlockSpec(a.shape, lambda i, d: (0, 0)) for a in (wpg, wple, g, b)]
        + [pl.BlockSpec(memory_space=pl.ANY)],
        out_specs=pl.BlockSpec((tm, D_MODEL), row),
        scratch_shapes=[pltpu.VMEM((2, tm, D_MODEL), F32), pltpu.SemaphoreType.DMA((2,))])
    return pl.pallas_call(
        functools.partial(_combine_kernel, tm=tm, n=n),
        grid_spec=grid_spec,
        out_shape=jax.ShapeDtypeStruct((n, D_MODEL), F32),
        compiler_params=_params("arbitrary"),
        name="combine",
    )(dest, x1, p, wt, wpg, wple, g, b, yb)


def _tile(n, pref):
    return pref if n % pref == 0 else n


def _rope_tables(pos):
    def tab(half, reps):
        freq = ROPE_THETA ** (-jnp.arange(half, dtype=F32) / half)
        ang = pos.astype(F32)[:, None] * freq[None, :]
        cos, sin = jnp.cos(ang), jnp.sin(ang)
        return jnp.tile(jnp.concatenate([cos, cos], -1), (1, reps)), jnp.tile(jnp.concatenate([-sin, sin], -1), (1, reps))
    return tab(HEAD_DIM // 2, 1) + tab(IDX_DIM // 2, 2)


def _split_w_in(w_in):
    sizes = (N_HEADS * HEAD_DIM, N_KV_HEADS * HEAD_DIM, N_KV_HEADS * HEAD_DIM, N_IDX_HEADS * IDX_DIM,
             N_IDX_HEADS, IDX_DIM, D_MODEL, D_MODEL, D_MODEL, 3 * D_MODEL)
    parts, o = [], 0
    for s in sizes:
        parts.append(w_in[:, o:o + s].astype(BF16))
        o += s
    wq, wk, wv, wiq, wiw, wik, wxr, wgr, wxq, wgz = parts
    wiw = jnp.pad(wiw, ((0, 0), (0, LANES - N_IDX_HEADS)))
    wik = jnp.concatenate([wik, wik], axis=1)
    return (wq, wk, wv, wiq, wiw, wik), (wxr, wgr, wxq), wgz


def _ffn(x1, p, lw, xs_init=None):
    n = x1.shape[0]
    tm = _tile(n, 512)
    ei, wt, counts = _router(x1, lw["w_rt"], lw["b_rt"], tm)
    counts = counts[:, 0].astype(I32)
    padded = (counts + SLOT_BLOCK - 1) // SLOT_BLOCK * SLOT_BLOCK
    pad_end = jnp.cumsum(padded)
    pad_start = pad_end - padded
    experts = jnp.arange(N_EXPERTS, dtype=I32)[:, None, None]
    start = jnp.sum(jnp.where(ei[None, 0:2] == experts, pad_start[:, None, None], 0), axis=0)
    dest = (start + ei[2:4]).reshape(-1)
    n_blocks = -(-2 * n // SLOT_BLOCK) + N_EXPERTS
    blk_start = jnp.arange(n_blocks, dtype=I32) * SLOT_BLOCK
    block_e = jnp.minimum(jnp.sum((pad_end[None, :] <= blk_start[:, None]).astype(I32), axis=1), N_EXPERTS - 1)
    n_used = (pad_end[-1:] // SLOT_BLOCK).astype(I32)
    xs = _dispatch(dest, x1, n_blocks * SLOT_BLOCK, tm, xs_init)
    yb = _experts(block_e, n_used, xs, lw["w_gate"], lw["w_up"], lw["w_down"], lw["layer"])
    tc = _tile(n, 256)
    return _combine(dest, x1, p, wt.T, lw["w_ple_gate"], lw["w_ple"], lw["ln2_g"], lw["ln2_b"], yb, tc), xs


def kernel(x_prompt, x_sample, p_prompt, p_sample, cache_k, cache_v, cache_idx, state_lru_h, state_lru_conv, state_pool, page_table, w_in, w_out, lru_conv_w, lru_conv_b, lru_wa, lru_ba, lru_wi, lru_bi, lru_lambda, pool_w, pool_scale, ln1_g, ln1_b, w_router_group, b_router_group, w_router_expert, b_router_expert, w_exp_gate, w_exp_up, w_exp_down, w_ple, w_ple_gate, ln2_g, ln2_b):
    bp, tp = x_prompt.shape[:2]
    bs, ts = x_sample.shape[:2]
    assert ts == 1, "the sample group decodes one token per sequence"
    depth = w_in.shape[0]
    past = page_table.shape[1] * PAGE_SIZE
    n_p = bp * tp
    tm_p = _tile(tp, 512)
    tq = _tile(tp, DSA_Q_BLOCK)
    y_att_p = slots_p = slots_s = None
    tabs_p = _rope_tables(jnp.arange(tp))
    tabs_s = _rope_tables(jnp.full((bs,), past, I32))
    cache_idx_t = jnp.swapaxes(cache_idx, 2, 3)

    xp = x_prompt.reshape(n_p, D_MODEL)
    xs = x_sample.reshape(bs, D_MODEL)
    outs = [[] for _ in range(12)]
    row2 = lambda a: a.reshape(1, -1)
    for i in range(depth):
        w_attn, w_mix, wgz = _split_w_in(w_in[i])
        mix_w = w_mix + (lru_conv_w[i], row2(lru_conv_b[i]), lru_wa[i].astype(BF16), row2(lru_ba[i]),
                         lru_wi[i].astype(BF16), row2(lru_bi[i]), row2(lru_lambda[i]),
                         pool_w[i].astype(BF16), row2(pool_scale[i]))
        wout = w_out[i].astype(BF16)
        g1, b1 = row2(ln1_g[i]), row2(ln1_b[i])
        w_rt = jnp.concatenate([w_router_expert[i].T, w_router_group[i].T,
                                jnp.zeros((ROUTER_ROWS - N_EXPERTS - N_EXPERT_GROUPS, D_MODEL), F32)], 0)
        b_rt = jnp.concatenate([b_router_expert[i], b_router_group[i],
                                jnp.zeros((ROUTER_ROWS - N_EXPERTS - N_EXPERT_GROUPS,), F32)])[:, None]
        lw = dict(w_rt=w_rt, b_rt=b_rt, w_gate=w_exp_gate, w_up=w_exp_up, w_down=w_exp_down, layer=i,
                  w_ple_gate=w_ple_gate[i].astype(BF16),
                  w_ple=w_ple[i].astype(BF16), ln2_g=row2(ln2_g[i]), ln2_b=row2(ln2_b[i]))

        q, k, v, kb, vb, iq, iw, ik = _proj_attn(xp, w_attn, tabs_p, tm_p, tp // tm_p)
        y_att_p = _dsa_prompt(q, kb, vb, iq, iw, ik, bp, tp, tq, y_att_p)
        y_lru, y_pool, h_new, conv_new, pool_new = _mix_prompt(xp.reshape(bp, tp, D_MODEL), mix_w, bp, tp)
        x1 = _merge(xp, y_att_p, y_lru.reshape(n_p, D_MODEL), y_pool.reshape(n_p, D_MODEL), wgz, wout, g1, b1,
                    _tile(n_p, 256))
        xp, slots_p = _ffn(x1, p_prompt[i].reshape(n_p, PLE_DIM), lw, slots_p)
        outs[0].append(k.reshape(bp, tp, N_KV_HEADS, HEAD_DIM))
        outs[1].append(v.reshape(bp, tp, N_KV_HEADS, HEAD_DIM))
        outs[2].append(ik[:, :IDX_DIM].reshape(bp, tp, IDX_DIM))
        outs[3].append(h_new)
        outs[4].append(conv_new[:, SUBLANES - (CONV_WIDTH - 1):])
        outs[5].append(pool_new[:, 2 * SUBLANES - POOL_BUF:])

        q, k, v, _, _, iq, iw, ik = _proj_attn(xs, w_attn, tabs_s, bs, 1)
        sel_idx = _dsa_sample_select(page_table, iq, iw, ik[:, :IDX_DIM], cache_idx_t, i)
        y_att = _dsa_sample_attend(sel_idx[:, :, 0], page_table, q, k, v, cache_k, cache_v, i)
        y_lru, y_pool, h_new, xr, xq = _mix_sample(xs, mix_w, state_lru_conv[i], state_lru_h[i], state_pool[i], past)
        x1 = _merge(xs, y_att, y_lru, y_pool, wgz, wout, g1, b1, bs)
        xs, slots_s = _ffn(x1, p_sample[i].reshape(bs, PLE_DIM), lw, slots_s)
        outs[6].append(k.reshape(bs, ts, N_KV_HEADS, HEAD_DIM))
        outs[7].append(v.reshape(bs, ts, N_KV_HEADS, HEAD_DIM))
        outs[8].append(ik[:, :IDX_DIM].reshape(bs, ts, IDX_DIM))
        outs[9].append(h_new)
        outs[10].append(jnp.concatenate([state_lru_conv[i][:, 1:], xr[:, None]], 1))
        outs[11].append(jnp.concatenate([state_pool[i][:, 1:], xq[:, None]], 1))

    return (xp.reshape(bp, tp, D_MODEL), xs.reshape(bs, ts, D_MODEL)) + tuple(jnp.stack(o) for o in outs)
```

```python
import functools

import jax
import jax.numpy as jnp
import numpy as np
from jax import lax
from jax.experimental import pallas as pl
from jax.experimental.pallas import tpu as pltpu

F32 = jnp.float32
BF16 = jnp.bfloat16
I32 = jnp.int32

D_MODEL = 1024
N_HEADS = 8
HEAD_DIM = 128
N_KV_HEADS = 4
KV_GROUP = N_HEADS // N_KV_HEADS
N_IDX_HEADS = 8
IDX_DIM = 64
INDEX_SCALE = (IDX_DIM * N_IDX_HEADS) ** -0.5
TOPK_MAX = 256
ROPE_THETA = 10000.0
PAGE_SIZE = 128
N_LRU_BLOCKS = 8
LRU_BLOCK = D_MODEL // N_LRU_BLOCKS
CONV_WIDTH = 4
LRU_C = 8.0
POOL_WINDOWS = (2, 4, 8, 16)
POOL_GROUP = D_MODEL // len(POOL_WINDOWS)
POOL_BUF = max(POOL_WINDOWS) - 1
N_EXPERT_GROUPS = 4
EXPERTS_PER_GROUP = 8
N_EXPERTS = N_EXPERT_GROUPS * EXPERTS_PER_GROUP
D_EXPERT = 512
PLE_DIM = 256
LN_EPS = 1e-5
DEPTH = 2
DEEPNORM_ALPHA = (2 * DEPTH) ** 0.25
ATTN_SCALE = HEAD_DIM ** -0.5

LANES = 128
SUBLANES = 8
SLOT_BLOCK = 256
DSA_EXTENTS = 8
DSA_Q_BLOCK = 256
RADIX4_MAX_ELEMS = 256 * 1024
SEQ_GROUP = 8
MIX_SEQS = 8
MIX_STEPS = 64
DMA_UNROLL = 8
VMEM_LIMIT = 48 * 1024 * 1024
INT_MIN = -2 ** 31
KEY_NEG_INF = INT_MIN + 0x7FFFFF

_NT = (((1,), (1,)), ((), ()))


def _dot(a, b):
    return jnp.dot(a, b, preferred_element_type=F32)


def _dot_nt(a, b):
    return lax.dot_general(a, b, _NT, preferred_element_type=F32)


def _sigmoid(x):
    return 1.0 / (1.0 + jnp.exp(-x))


def _gelu_tanh(x):
    c = np.float32(np.sqrt(2.0 / np.pi))
    return x * (0.5 * (1.0 + jnp.tanh(c * (x + 0.044715 * (x * x * x)))))


def _softplus(x):
    return jnp.maximum(x, 0.0) + jnp.log1p(jnp.exp(-jnp.abs(x)))


def _layer_norm(r, g, b):
    mu = jnp.mean(r, axis=-1, keepdims=True)
    c = r - mu
    var = jnp.mean(c * c, axis=-1, keepdims=True)
    return c * lax.rsqrt(var + LN_EPS) * g + b


def _sort_key(x):
    bits = pltpu.bitcast(jnp.where(x == 0.0, 0.0, x), I32)
    return jnp.where(bits >= 0, bits, bits ^ 0x7FFFFFFF)


def _params(*sem):
    return pltpu.CompilerParams(dimension_semantics=sem, vmem_limit_bytes=VMEM_LIMIT)


def _full(shape):
    n = len(shape)
    return pl.BlockSpec(shape, lambda *_: (0,) * n)


def _proj_attn_kernel(x_ref, wq, wk, wv, wiq, wiw, wik, cos_ref, sin_ref, cosi_ref, sini_ref,
                      q_o, k_o, v_o, kb_o, vb_o, iq_o, iw_o, ik_o):
    xb = x_ref[...].astype(BF16)
    cos, sin = cos_ref[...], sin_ref[...]
    cosi, sini = cosi_ref[...], sini_ref[...]
    lane = lax.broadcasted_iota(I32, cos.shape, 1)
    low_half = (lane & (IDX_DIM - 1)) < (IDX_DIM // 2)

    def rope128(z):
        return z * cos + pltpu.roll(z, HEAD_DIM // 2, 1) * sin

    def rope64(z):
        partner = jnp.where(low_half, pltpu.roll(z, LANES - IDX_DIM // 2, 1), pltpu.roll(z, IDX_DIM // 2, 1))
        return z * cosi + partner * sini

    q = _dot(xb, wq[...])
    for h in range(N_HEADS):
        sl = slice(h * LANES, (h + 1) * LANES)
        q_o[:, sl] = (rope128(q[:, sl]) * ATTN_SCALE).astype(BF16)
    k = _dot(xb, wk[...])
    for c in range(N_KV_HEADS):
        sl = slice(c * LANES, (c + 1) * LANES)
        kr = rope128(k[:, sl])
        k_o[:, sl] = kr
        kb_o[:, sl] = kr.astype(BF16)
    v = _dot(xb, wv[...])
    v_o[...] = v
    vb_o[...] = v.astype(BF16)
    iq = _dot(xb, wiq[...])
    for j in range(N_IDX_HEADS * IDX_DIM // LANES):
        sl = slice(j * LANES, (j + 1) * LANES)
        iq_o[:, sl] = rope64(iq[:, sl])
    iw_o[...] = _dot(xb, wiw[...])
    ik_o[...] = rope64(_dot(xb, wik[...]))


def _proj_attn(x, w, tabs, tm, n_tab_blocks):
    n = x.shape[0]
    cos, sin, cosi, sini = tabs
    row = lambda i: (i, 0)
    tab = lambda i: (i % n_tab_blocks, 0)
    wspec = lambda a: _full(a.shape)
    outs = [
        ((n, 1024), BF16), ((n, 512), F32), ((n, 512), F32), ((n, 512), BF16), ((n, 512), BF16),
        ((n, 512), F32), ((n, LANES), F32), ((n, LANES), F32)]
    return pl.pallas_call(
        _proj_attn_kernel,
        grid=(n // tm,),
        in_specs=[pl.BlockSpec((tm, D_MODEL), row)] + [wspec(a) for a in w]
        + [pl.BlockSpec((tm, LANES), tab)] * 4,
        out_specs=[pl.BlockSpec((tm, s[1]), row) for s, _ in outs],
        out_shape=[jax.ShapeDtypeStruct(s, d) for s, d in outs],
        compiler_params=_params("parallel"),
        name="proj_attn",
    )(x, *w, cos, sin, cosi, sini)


def _count_rows(mask):
    return jnp.sum(jnp.where(mask, 1.0, 0.0), axis=1, keepdims=True)


def _dsa_prompt_kernel(q_ref, kb_ref, vb_ref, iq_ref, iw_ref, ik_ref, *rest, tq, t_len, q0, topk, idx_bits):
    o_ref, p_sc = rest[-2:]
    qi = q0 + pl.program_id(1)
    ikb = ik_ref[...].astype(BF16)
    iq = iq_ref[...]
    iw = iw_ref[...] * INDEX_SCALE
    lane = lax.broadcasted_iota(I32, (tq, LANES), 1)
    score = jnp.zeros((tq, t_len), F32)

    for h in range(N_IDX_HEADS):
        chunk = iq[:, (h // 2) * LANES:(h // 2 + 1) * LANES]
        keep = (lane < IDX_DIM) if h % 2 == 0 else (lane >= IDX_DIM)
        s = _dot_nt(jnp.where(keep, chunk, 0.0).astype(BF16), ikb)
        score = score + jnp.maximum(s, 0.0) * iw[:, h:h + 1]
    qpos = qi * tq + lax.broadcasted_iota(I32, (tq, 1), 0)
    kpos = lax.broadcasted_iota(I32, (tq, t_len), 1)
    causal = kpos <= qpos
    key = _sort_key(jnp.where(causal, score, -jnp.inf))

    if tq * t_len <= RADIX4_MAX_ELEMS:
        def thr_body(b, t):
            step = lax.shift_left(jnp.int32(1), 30 - 2 * b)
            cands = [t + step, t + 2 * step, t + 3 * step]
            ok = [_count_rows(key >= c) >= topk for c in cands]
            return jnp.where(ok[2], cands[2], jnp.where(ok[1], cands[1], jnp.where(ok[0], cands[0], t)))
        n_steps = 16
    else:
        def thr_body(b, t):
            cand = t + lax.shift_left(jnp.int32(1), 31 - b)
            return jnp.where(_count_rows(key >= cand) >= topk, cand, t)
        n_steps = 32

    thr = lax.fori_loop(0, n_steps, thr_body, jnp.full((tq, 1), INT_MIN, I32))
    gt = key > thr
    eq = key == thr
    need = topk - _count_rows(gt)
    tie = (_count_rows(eq) > need) & (thr > KEY_NEG_INF)
    p_sc[...] = jnp.full((tq, 1), t_len, I32)

    @pl.when(jnp.max(jnp.where(tie, 1.0, 0.0)) > 0.0)
    def _():
        def pos_body(b, p):
            cand = p + lax.shift_left(jnp.int32(1), idx_bits - 1 - b)
            return jnp.where(_count_rows(eq & (kpos < cand)) < need, cand, p)
        p_sc[...] = lax.fori_loop(0, idx_bits, pos_body, jnp.zeros((tq, 1), I32))

    sel = causal & (gt | (eq & (kpos <= p_sc[...])))
    bias = jnp.where(sel, 0.0, -jnp.inf)
    for c in range(N_KV_HEADS):
        kc = kb_ref[:, c * LANES:(c + 1) * LANES]
        vc = vb_ref[:, c * LANES:(c + 1) * LANES]
        vx = jnp.concatenate([vc, jnp.ones_like(vc)], axis=1)
        for g in range(KV_GROUP):
            h = c * KV_GROUP + g
            s = _dot_nt(q_ref[:, h * LANES:(h + 1) * LANES], kc) + bias
            m = jnp.max(s, axis=1, keepdims=True)
            pv = _dot(jnp.exp(s - m).astype(BF16), vx)
            o_ref[:, h * LANES:(h + 1) * LANES] = pv[:, 0:LANES] / pv[:, LANES:LANES + 1]


def _dsa_prompt(q, kb, vb, iq, iw, ik, batch, t_len, tq, y_init=None):
    topk = min(TOPK_MAX, t_len // 4)
    nq = t_len // tq
    n_var = max(d for d in range(1, DSA_EXTENTS + 1) if nq % d == 0)
    per = nq // n_var
    kb, vb, ik = (a.reshape(batch, t_len, a.shape[-1]) for a in (kb, vb, ik))
    y = jnp.zeros((batch * t_len, 1024), F32) if y_init is None else y_init
    for v in range(n_var):
        ext = (v + 1) * per * tq
        qrow = lambda b, i, v=v: (b * nq + v * per + i, 0)
        brow = lambda b, i: (b, 0, 0)
        in_specs = [pl.BlockSpec((tq, 1024), qrow), pl.BlockSpec((None, ext, 512), brow),
                    pl.BlockSpec((None, ext, 512), brow), pl.BlockSpec((tq, 512), qrow),
                    pl.BlockSpec((tq, LANES), qrow), pl.BlockSpec((None, ext, LANES), brow)]
        in_specs.append(pl.BlockSpec(memory_space=pl.ANY))
        y = pl.pallas_call(
            functools.partial(_dsa_prompt_kernel, tq=tq, t_len=ext, q0=v * per, topk=topk,
                              idx_bits=max(1, int(np.ceil(np.log2(ext))))),
            grid=(batch, per),
            in_specs=in_specs,
            out_specs=pl.BlockSpec((tq, 1024), qrow),
            out_shape=jax.ShapeDtypeStruct((batch * t_len, 1024), F32),
            scratch_shapes=[pltpu.VMEM((tq, 1), I32)],
            input_output_aliases={6: 0},
            compiler_params=_params("parallel", "arbitrary"),
            name="dsa_prompt",
        )(q, kb, vb, iq, iw, ik, y)
    return y


def _dsa_sample_select_kernel(pt_ref, iq_ref, iw_ref, ikn_ref, cidx_ref, idx_o,
                              ikbuf, s_sc, sem, *, layer, n_pages, topk, idx_bits, group):
    g0 = pl.program_id(0) * group
    past = n_pages * PAGE_SIZE

    def page_copy(s, j):
        slot = s % 2
        return pltpu.make_async_copy(cidx_ref.at[layer, pt_ref[(g0 + s) * n_pages + j]],
                                     ikbuf.at[slot, j], sem.at[slot])

    def issue_seq(s):
        def body(j, c):
            page_copy(s, j).start()
            return c
        lax.fori_loop(0, n_pages, body, 0, unroll=DMA_UNROLL)

    def drain_seq(s):
        def body(j, c):
            page_copy(s, j).wait()
            return c
        lax.fori_loop(0, n_pages, body, 0, unroll=DMA_UNROLL)

    issue_seq(0)

    def seq_body(s, c):
        @pl.when(s + 1 < group)
        def _():
            issue_seq(s + 1)
        drain_seq(s)
        slot = s % 2
        iq8 = iq_ref[s].astype(BF16)
        w8 = iw_ref[s] * INDEX_SCALE
        for j in range(n_pages):
            s8 = _dot(iq8, ikbuf[slot, j].astype(BF16))
            s_sc[s, j:j + 1, :] = jnp.sum(jnp.maximum(s8, 0.0) * w8, axis=0, keepdims=True)
        return c

    lax.fori_loop(0, group, seq_body, 0)

    w_all = iw_ref[...] * INDEX_SCALE
    own = jnp.sum(iq_ref[...].astype(BF16).astype(F32) * ikn_ref[...].astype(BF16).astype(F32),
                  axis=2, keepdims=True)
    own = jnp.sum(jnp.maximum(own, 0.0) * w_all, axis=1, keepdims=True)

    key = _sort_key(s_sc[...])
    key_own = _sort_key(own)
    pos = (lax.broadcasted_iota(I32, key.shape, 1) * PAGE_SIZE + lax.broadcasted_iota(I32, key.shape, 2))

    def count(mask, mask_own):
        c = jnp.sum(jnp.where(mask, 1.0, 0.0), axis=1, keepdims=True)
        return jnp.sum(c, axis=2, keepdims=True) + jnp.where(mask_own, 1.0, 0.0)

    def thr_body(i, t):
        cand = t + lax.shift_left(jnp.int32(1), 31 - i)
        return jnp.where(count(key >= cand, key_own >= cand) >= topk, cand, t)

    thr = lax.fori_loop(0, 32, thr_body, jnp.full((group, 1, 1), INT_MIN, I32))
    gt, eq = key > thr, key == thr
    gt_own, eq_own = key_own > thr, key_own == thr
    need = topk - count(gt, gt_own)

    def pos_body(i, p):
        cand = p + lax.shift_left(jnp.int32(1), idx_bits - 1 - i)
        return jnp.where(count(eq & (pos < cand), eq_own & (past < cand)) < need, cand, p)

    plast = lax.fori_loop(0, idx_bits, pos_body, jnp.zeros((group, 1, 1), I32))
    s_sc[...] = jnp.where(gt | (eq & (pos <= plast)), 1.0, 0.0)

    r_i = lax.broadcasted_iota(I32, (PAGE_SIZE, PAGE_SIZE), 0)
    c_i = lax.broadcasted_iota(I32, (PAGE_SIZE, PAGE_SIZE), 1)
    tri_incl = jnp.where(r_i <= c_i, 1.0, 0.0).astype(BF16)
    pr = lax.broadcasted_iota(I32, (n_pages, n_pages), 0)
    pc = lax.broadcasted_iota(I32, (n_pages, n_pages), 1)
    tri_pages = jnp.where(pr <= pc, 1.0, 0.0).astype(BF16)
    ones_rows = jnp.ones((SUBLANES, PAGE_SIZE), BF16)
    rank = lax.broadcasted_iota(I32, (topk, 1), 0).astype(F32)
    page_lane = lax.broadcasted_iota(I32, (topk, n_pages), 1).astype(F32)

    def compact(s, c):
        selb = s_sc[s].astype(BF16)
        within = _dot(selb, tri_incl)
        n_row = _dot_nt(ones_rows, selb)[0:1]
        end_row = _dot(jnp.broadcast_to(n_row, (SUBLANES, n_pages)).astype(BF16), tri_pages)[0:1]
        before = end_row <= rank
        page_of = jnp.sum(jnp.where(before, 1.0, 0.0), axis=1, keepdims=True)
        local = rank - jnp.sum(jnp.where(before, n_row, 0.0), axis=1, keepdims=True)
        page_cum = _dot(jnp.where(page_lane == page_of, 1.0, 0.0).astype(BF16), within.astype(BF16))
        lane_of = jnp.sum(jnp.where(page_cum <= local, 1.0, 0.0), axis=1, keepdims=True)
        idx_o[s] = jnp.minimum(page_of * PAGE_SIZE + lane_of, float(past)).astype(I32)
        return c

    lax.fori_loop(0, group, compact, 0)


def _dsa_sample_select(page_table, iq, iw, ik_new, cache_idx_t, layer):
    bd, n_pages = page_table.shape
    past = n_pages * PAGE_SIZE
    topk = min(TOPK_MAX, (past + 1) // 4)
    idx_bits = int(np.floor(np.log2(past))) + 1
    group = SEQ_GROUP if bd % SEQ_GROUP == 0 else bd
    grp = lambda g, pt: (g, 0, 0)
    grid_spec = pltpu.PrefetchScalarGridSpec(
        num_scalar_prefetch=1,
        grid=(bd // group,),
        in_specs=[pl.BlockSpec((group, N_IDX_HEADS, IDX_DIM), grp),
                  pl.BlockSpec((group, N_IDX_HEADS, 1), grp),
                  pl.BlockSpec((group, 1, IDX_DIM), grp),
                  pl.BlockSpec(memory_space=pl.ANY)],
        out_specs=pl.BlockSpec((group, topk, 1), grp),
        scratch_shapes=[pltpu.VMEM((2, n_pages, IDX_DIM, PAGE_SIZE), F32), pltpu.VMEM((group, n_pages, PAGE_SIZE), F32),
                        pltpu.SemaphoreType.DMA((2,))])
    return pl.pallas_call(
        functools.partial(_dsa_sample_select_kernel, layer=layer, n_pages=n_pages, topk=topk, idx_bits=idx_bits,
                          group=group),
        grid_spec=grid_spec,
        out_shape=jax.ShapeDtypeStruct((bd, topk, 1), I32),
        compiler_params=_params("arbitrary"),
        name="dsa_sample_select",
    )(page_table.reshape(-1), iq.reshape(bd, N_IDX_HEADS, IDX_DIM), iw[:, :N_IDX_HEADS, None],
      ik_new[:, None, :], cache_idx_t)


def _dsa_sample_attend_kernel(idx_ref, pt_ref, q_ref, kn_ref, vn_ref, idxv_ref, ck_ref, cv_ref, o_ref,
                              kbuf0, vbuf0, kbuf1, vbuf1, sem, *, layer, n_pages, topk):
    b = pl.program_id(0)
    last = pl.num_programs(0) - 1
    past = n_pages * PAGE_SIZE
    bufs = ((kbuf0, vbuf0), (kbuf1, vbuf1))

    def row_copies(seq, buf, r):
        pidx = jnp.minimum(idx_ref[seq * topk + r], past - 1)
        phys = pt_ref[seq * n_pages + pidx // PAGE_SIZE]
        off = pidx % PAGE_SIZE
        return (pltpu.make_async_copy(ck_ref.at[layer, phys, off], bufs[buf][0].at[r], sem.at[buf, 0]),
                pltpu.make_async_copy(cv_ref.at[layer, phys, off], bufs[buf][1].at[r], sem.at[buf, 1]))

    def issue_all(seq, buf):
        for r in range(topk):
            for cp in row_copies(seq, buf, r):
                cp.start()

    def drain_all(seq, buf):
        def body(r, c):
            for cp in row_copies(seq, buf, r):
                cp.wait()
            return c
        lax.fori_loop(0, topk, body, 0, unroll=DMA_UNROLL)

    @pl.when(b == 0)
    def _():
        issue_all(0, 0)

    def step(buf):
        drain_all(b, buf)
        nxt = jnp.minimum(b + 1, last)
        issue_all(nxt, 1 - buf)
        own = idxv_ref[...] >= past
        k_sel = jnp.where(own, kn_ref[...][None], bufs[buf][0][...])
        v_sel = jnp.where(own, vn_ref[...][None], bufs[buf][1][...])
        for g in range(KV_GROUP):
            s = jnp.sum(k_sel * q_ref[g][None], axis=-1, keepdims=True)
            m = jnp.max(s, axis=0, keepdims=True)
            p = jnp.exp(s - m)
            l = jnp.sum(p, axis=0)
            o_ref[g] = jnp.sum(p * v_sel, axis=0) / l

        @pl.when(b == last)
        def _():
            drain_all(nxt, 1 - buf)

    for parity in range(2):
        pl.when(b % 2 == parity)(functools.partial(step, parity))


def _dsa_sample_attend(idx, page_table, q, k_new, v_new, cache_k, cache_v, layer):
    bd, n_pages = page_table.shape
    topk = idx.shape[1]
    qg = q.astype(F32).reshape(bd, N_KV_HEADS, KV_GROUP, HEAD_DIM).transpose(0, 2, 1, 3)
    head = lambda b, *_: (b, 0, 0)
    grid_spec = pltpu.PrefetchScalarGridSpec(
        num_scalar_prefetch=2,
        grid=(bd,),
        in_specs=[pl.BlockSpec((None, KV_GROUP, N_KV_HEADS, HEAD_DIM), lambda b, *_: (b, 0, 0, 0)),
                  pl.BlockSpec((None, N_KV_HEADS, HEAD_DIM), head),
                  pl.BlockSpec((None, N_KV_HEADS, HEAD_DIM), head),
                  pl.BlockSpec((None, topk, 1, 1), lambda b, *_: (b, 0, 0, 0)),
                  pl.BlockSpec(memory_space=pl.ANY), pl.BlockSpec(memory_space=pl.ANY)],
        out_specs=pl.BlockSpec((None, KV_GROUP, N_KV_HEADS, HEAD_DIM), lambda b, *_: (b, 0, 0, 0)),
        scratch_shapes=[pltpu.VMEM((topk, N_KV_HEADS, HEAD_DIM), F32)] * 4 + [pltpu.SemaphoreType.DMA((2, 2))])
    o = pl.pallas_call(
        functools.partial(_dsa_sample_attend_kernel, layer=layer, n_pages=n_pages, topk=topk),
        grid_spec=grid_spec,
        out_shape=jax.ShapeDtypeStruct((bd, KV_GROUP, N_KV_HEADS, HEAD_DIM), F32),
        compiler_params=_params("arbitrary"),
        name="dsa_sample_attend",
    )(idx.reshape(-1), page_table.reshape(-1), qg, k_new.reshape(bd, N_KV_HEADS, HEAD_DIM),
      v_new.reshape(bd, N_KV_HEADS, HEAD_DIM), idx.reshape(bd, topk, 1, 1), cache_k, cache_v)
    return o.transpose(0, 2, 1, 3).reshape(bd, N_HEADS * HEAD_DIM)


def _lru_gates(xc, wa, ba, wi, bi, lam):
    xcb = xc.astype(BF16)
    a_parts, u_parts = [], []
    for n in range(N_LRU_BLOCKS):
        sl = slice(n * LRU_BLOCK, (n + 1) * LRU_BLOCK)
        r = _sigmoid(_dot(xcb[:, sl], wa[n]) + ba[:, sl])
        i = _sigmoid(_dot(xcb[:, sl], wi[n]) + bi[:, sl])
        log_a = -LRU_C * r * _softplus(-lam[:, sl])
        a_parts.append(jnp.exp(log_a))
        th = jnp.tanh(log_a)
        u_parts.append(jnp.sqrt(-2.0 * th / (1.0 - th)) * i * xc[:, sl])
    return a_parts, u_parts


def _pool_mix(window_sum, xq, cnt, pool_w, pool_scale, g):
    sl = slice(g * POOL_GROUP, (g + 1) * POOL_GROUP)
    pooled = window_sum / cnt - xq[:, sl]
    return _dot(pooled.astype(BF16), pool_w[g]) * pool_scale[:, sl]


def _mix_prompt_kernel(x_ref, wxr, wgr, wxq, convw, convb, wa, ba, wi, bi, lam, poolw, pscale,
                       ylru_o, ypool_o, h_o, conv_o, pool_o,
                       xr_ext, xq_ext, a_sc, u_sc, h_sc, h_carry, *, nb, tt):
    t = pl.program_id(1)
    halo_r, halo_q = SUBLANES, 2 * SUBLANES
    rows = nb * tt

    @pl.when(t == 0)
    def _():
        xr_ext[:, 0:halo_r] = jnp.zeros((nb, halo_r, D_MODEL), F32)
        xq_ext[:, 0:halo_q] = jnp.zeros((nb, halo_q, D_MODEL), F32)
        h_carry[...] = jnp.zeros_like(h_carry)

    @pl.when(t > 0)
    def _():
        xr_ext[:, 0:halo_r] = xr_ext[:, tt:tt + halo_r]
        xq_ext[:, 0:halo_q] = xq_ext[:, tt:tt + halo_q]

    xb = x_ref[...].reshape(rows, D_MODEL).astype(BF16)
    xr3 = _dot(xb, wxr[...]).reshape(nb, tt, D_MODEL)
    xq3 = _dot(xb, wxq[...]).reshape(nb, tt, D_MODEL)
    xr_ext[:, halo_r:halo_r + tt] = xr3
    xq_ext[:, halo_q:halo_q + tt] = xq3

    cw = convw[...]
    xc = convb[...] + cw[CONV_WIDTH - 1:CONV_WIDTH] * xr3
    for j in range(CONV_WIDTH - 1):
        o = halo_r - (CONV_WIDTH - 1) + j
        xc = xc + cw[j:j + 1] * xr_ext[:, o:o + tt]
    a_parts, u_parts = _lru_gates(xc.reshape(rows, D_MODEL), wa, ba[...], wi, bi[...], lam[...])
    for n in range(N_LRU_BLOCKS):
        sl = slice(n * LRU_BLOCK, (n + 1) * LRU_BLOCK)
        a_sc[:, sl] = a_parts[n]
        u_sc[:, sl] = u_parts[n]

    a3 = pltpu.einshape("btd->tbd", a_sc[...].reshape(nb, tt, D_MODEL))
    u3 = pltpu.einshape("btd->tbd", u_sc[...].reshape(nb, tt, D_MODEL))
    h = h_carry[...]
    for s in range(tt):
        h = a3[s] * h + u3[s]
        h_sc[s] = h
    h_carry[...] = h
    hs = pltpu.einshape("tbd->btd", h_sc[...]).reshape(rows, D_MODEL)
    ylru_o[...] = (hs * _gelu_tanh(_dot(xb, wgr[...]))).reshape(nb, tt, D_MODEL)

    posn = t * tt + lax.broadcasted_iota(I32, (1, tt, 1), 1)
    for g, w in enumerate(POOL_WINDOWS):
        sl = slice(g * POOL_GROUP, (g + 1) * POOL_GROUP)
        acc = xq_ext[:, :, sl]
        d = 1
        while d < w:
            acc = acc + pltpu.roll(acc, d, 1)
            d *= 2
        cnt = jnp.minimum(posn + 1, w).astype(F32)
        pooled = (acc[:, halo_q:] / cnt - xq3[:, :, sl]).reshape(rows, POOL_GROUP)
        mixed = _dot(pooled.astype(BF16), poolw[g]) * pscale[:, sl]
        ypool_o[:, :, sl] = mixed.reshape(nb, tt, POOL_GROUP)

    @pl.when(t == pl.num_programs(1) - 1)
    def _():
        h_o[...] = h
        conv_o[...] = xr_ext[:, tt:tt + halo_r]
        pool_o[...] = xq_ext[:, tt:tt + halo_q]


def _mix_prompt(x, w, batch, t_len):
    nb = _tile(batch, MIX_SEQS)
    tt = _tile(t_len, MIX_STEPS)
    blk = pl.BlockSpec((nb, tt, D_MODEL), lambda b, t: (b, t, 0))
    per_b = lambda b, t: (b, 0, 0)
    outs = [((batch, t_len, D_MODEL), F32), ((batch, t_len, D_MODEL), F32),
            ((batch, D_MODEL), F32), ((batch, SUBLANES, D_MODEL), F32), ((batch, 2 * SUBLANES, D_MODEL), F32)]
    return pl.pallas_call(
        functools.partial(_mix_prompt_kernel, nb=nb, tt=tt),
        grid=(batch // nb, t_len // tt),
        in_specs=[blk] + [_full(a.shape) for a in w],
        out_specs=[blk, blk, pl.BlockSpec((nb, D_MODEL), lambda b, t: (b, 0)),
                   pl.BlockSpec((nb, SUBLANES, D_MODEL), per_b), pl.BlockSpec((nb, 2 * SUBLANES, D_MODEL), per_b)],
        out_shape=[jax.ShapeDtypeStruct(s, d) for s, d in outs],
        scratch_shapes=[pltpu.VMEM((nb, tt + SUBLANES, D_MODEL), F32), pltpu.VMEM((nb, tt + 2 * SUBLANES, D_MODEL), F32),
                        pltpu.VMEM((nb * tt, D_MODEL), F32), pltpu.VMEM((nb * tt, D_MODEL), F32),
                        pltpu.VMEM((tt, nb, D_MODEL), F32), pltpu.VMEM((nb, D_MODEL), F32)],
        compiler_params=_params("parallel", "arbitrary"),
        name="mix_prompt",
    )(x, *w)


def _mix_sample_kernel(x_ref, wxr, wgr, wxq, convw, convb, wa, ba, wi, bi, lam, poolw, pscale,
                       conv_ref, h_ref, pool_ref, ylru_o, ypool_o, h_o, xr_o, xq_o, *, cnt_pos):
    xb = x_ref[...].astype(BF16)
    xr = _dot(xb, wxr[...])
    xq = _dot(xb, wxq[...])
    cw = convw[...]
    xc = convb[...] + cw[CONV_WIDTH - 1:CONV_WIDTH] * xr
    for j in range(CONV_WIDTH - 1):
        xc = xc + cw[j:j + 1] * conv_ref[j]
    a_parts, u_parts = _lru_gates(xc, wa, ba[...], wi, bi[...], lam[...])
    gate = _gelu_tanh(_dot(xb, wgr[...]))
    for n in range(N_LRU_BLOCKS):
        sl = slice(n * LRU_BLOCK, (n + 1) * LRU_BLOCK)
        h = a_parts[n] * h_ref[:, sl] + u_parts[n]
        h_o[:, sl] = h
        ylru_o[:, sl] = h * gate[:, sl]
    for g, w in enumerate(POOL_WINDOWS):
        sl = slice(g * POOL_GROUP, (g + 1) * POOL_GROUP)
        acc = xq[:, sl]
        for j in range(1, w):
            acc = acc + pool_ref[POOL_BUF - j, :, sl]
        ypool_o[:, sl] = _pool_mix(acc, xq, float(min(cnt_pos, w)), poolw, pscale[...], g)
    xr_o[...] = xr
    xq_o[...] = xq


def _mix_sample(x, w, conv_state, h_state, pool_state, past):
    bd = x.shape[0]
    args = (x, *w, conv_state.transpose(1, 0, 2), h_state, pool_state.transpose(1, 0, 2))
    shp = jax.ShapeDtypeStruct((bd, D_MODEL), F32)
    return pl.pallas_call(
        functools.partial(_mix_sample_kernel, cnt_pos=past + 1),
        in_specs=[_full(a.shape) for a in args],
        out_specs=[_full((bd, D_MODEL))] * 5,
        out_shape=[shp] * 5,
        grid=(1,),
        compiler_params=_params("arbitrary"),
        name="mix_sample",
    )(*args)


def _merge_kernel(x_ref, ya_ref, yl_ref, yp_ref, wgz, wout, g_ref, b_ref, x1_o):
    x = x_ref[...]
    gz = _dot(x.astype(BF16), wgz[...])
    merged = (_sigmoid(gz[:, 0:D_MODEL]) * ya_ref[...] + _sigmoid(gz[:, D_MODEL:2 * D_MODEL]) * yl_ref[...]
              + _sigmoid(gz[:, 2 * D_MODEL:3 * D_MODEL]) * yp_ref[...])
    r = DEEPNORM_ALPHA * x + _dot(merged.astype(BF16), wout[...])
    x1_o[...] = _layer_norm(r, g_ref[...], b_ref[...])


def _merge(x, ya, yl, yp, wgz, wout, g, b, tm):
    n = x.shape[0]
    row = pl.BlockSpec((tm, D_MODEL), lambda i: (i, 0))
    return pl.pallas_call(
        _merge_kernel,
        grid=(n // tm,),
        in_specs=[row] * 4 + [_full(wgz.shape), _full(wout.shape), _full(g.shape), _full(b.shape)],
        out_specs=row,
        out_shape=jax.ShapeDtypeStruct((n, D_MODEL), F32),
        compiler_params=_params("parallel"),
        name="merge",
    )(x, ya, yl, yp, wgz, wout, g, b)


ROUTER_ROWS = 40


def _first_argmax(v, n):
    m = jnp.max(v, axis=0, keepdims=True)
    rows = lax.broadcasted_iota(I32, v.shape, 0)
    return m, jnp.min(jnp.where(v == m, rows, n), axis=0, keepdims=True)


def _router_kernel(x_ref, w_ref, b_ref, ei_o, wt_o, cnt_o, carry, *, tm):
    i = pl.program_id(0)

    @pl.when(i == 0)
    def _():
        carry[...] = jnp.zeros_like(carry)

    def split(v):
        hi = v.astype(BF16)
        return hi, (v - hi.astype(F32)).astype(BF16)

    x_hi, x_lo = split(x_ref[...])
    w_hi, w_lo = split(w_ref[...])
    logits = _dot_nt(w_hi, x_hi) + (_dot_nt(w_hi, x_lo) + _dot_nt(w_lo, x_hi)) + b_ref[...]
    le = logits[0:N_EXPERTS]
    lg = logits[N_EXPERTS:N_EXPERTS + N_EXPERT_GROUPS]

    gmax, gidx = _first_argmax(lg, N_EXPERT_GROUPS)
    p_top = 1.0 / jnp.sum(jnp.exp(lg - gmax), axis=0, keepdims=True)
    le_g = jnp.zeros((EXPERTS_PER_GROUP, tm), F32)
    for gi in range(N_EXPERT_GROUPS):
        le_g = le_g + jnp.where(gidx == gi, le[gi * EXPERTS_PER_GROUP:(gi + 1) * EXPERTS_PER_GROUP], 0.0)
    m1, i1 = _first_argmax(le_g, EXPERTS_PER_GROUP)
    rows8 = lax.broadcasted_iota(I32, le_g.shape, 0)
    m2, i2 = _first_argmax(jnp.where(rows8 == i1, -jnp.inf, le_g), EXPERTS_PER_GROUP)
    z = jnp.sum(jnp.exp(le_g - m1), axis=0, keepdims=True)
    p1 = 1.0 / z
    p2 = jnp.exp(m2 - m1) / z
    e1 = gidx * EXPERTS_PER_GROUP + i1
    e2 = gidx * EXPERTS_PER_GROUP + i2

    rows = lax.broadcasted_iota(I32, (N_EXPERTS, tm), 0)
    hit1, hit2 = rows == e1, rows == e2
    onehot = jnp.where(hit1 | hit2, 1.0, 0.0)
    r_i = lax.broadcasted_iota(I32, (tm, tm), 0)
    c_i = lax.broadcasted_iota(I32, (tm, tm), 1)
    before = _dot(onehot.astype(BF16), jnp.where(r_i < c_i, 1.0, 0.0).astype(BF16)) + carry[...]
    carry[...] = carry[...] + jnp.sum(onehot, axis=1, keepdims=True)

    ei_o[0:1, :] = e1
    ei_o[1:2, :] = e2
    ei_o[2:3, :] = jnp.sum(jnp.where(hit1, before, 0.0), axis=0, keepdims=True).astype(I32)
    ei_o[3:4, :] = jnp.sum(jnp.where(hit2, before, 0.0), axis=0, keepdims=True).astype(I32)
    ei_o[4:SUBLANES, :] = jnp.zeros((SUBLANES - 4, tm), I32)
    wt_o[0:1, :] = p1 / (p1 + p2) * p_top
    wt_o[1:2, :] = p2 / (p1 + p2) * p_top
    wt_o[2:SUBLANES, :] = jnp.zeros((SUBLANES - 2, tm), F32)
    cnt_o[...] = carry[...]


def _router(x1, w_rt, b_rt, tm):
    n = x1.shape[0]
    col = pl.BlockSpec((SUBLANES, tm), lambda i: (0, i))
    return pl.pallas_call(
        functools.partial(_router_kernel, tm=tm),
        grid=(n // tm,),
        in_specs=[pl.BlockSpec((tm, D_MODEL), lambda i: (i, 0)), _full(w_rt.shape), _full(b_rt.shape)],
        out_specs=[col, col, _full((N_EXPERTS, 1))],
        out_shape=[jax.ShapeDtypeStruct((SUBLANES, n), I32), jax.ShapeDtypeStruct((SUBLANES, n), F32),
                   jax.ShapeDtypeStruct((N_EXPERTS, 1), F32)],
        scratch_shapes=[pltpu.VMEM((N_EXPERTS, 1), F32)],
        compiler_params=_params("arbitrary"),
        name="router",
    )(x1, w_rt, b_rt)


def _dispatch_kernel(dest_ref, x_ref, xs_in, xs_out, sem, *, tm, n):
    del xs_in
    i = pl.program_id(0)

    def row_copy(r, k):
        return pltpu.make_async_copy(x_ref.at[pl.ds(r, 1)], xs_out.at[pl.ds(dest_ref[k * n + i * tm + r], 1)], sem)

    def issue(r, c):
        row_copy(r, 0).start()
        row_copy(r, 1).start()
        return c

    def drain(r, c):
        row_copy(r, 0).wait()
        row_copy(r, 1).wait()
        return c

    lax.fori_loop(0, tm, issue, 0, unroll=DMA_UNROLL)
    lax.fori_loop(0, tm, drain, 0, unroll=DMA_UNROLL)


def _dispatch(dest, x1, n_slots, tm, xs_init=None):
    n = x1.shape[0]
    if xs_init is None:
        xs_init = jnp.zeros((n_slots, D_MODEL), F32)
    grid_spec = pltpu.PrefetchScalarGridSpec(
        num_scalar_prefetch=1,
        grid=(n // tm,),
        in_specs=[pl.BlockSpec((tm, D_MODEL), lambda i, d: (i, 0)), pl.BlockSpec(memory_space=pl.ANY)],
        out_specs=pl.BlockSpec(memory_space=pl.ANY),
        scratch_shapes=[pltpu.SemaphoreType.DMA])
    return pl.pallas_call(
        functools.partial(_dispatch_kernel, tm=tm, n=n),
        grid_spec=grid_spec,
        out_shape=jax.ShapeDtypeStruct((n_slots, D_MODEL), F32),
        input_output_aliases={2: 0},
        compiler_params=_params("arbitrary"),
        name="dispatch",
    )(dest, x1, xs_init)


def _expert_kernel(be_ref, nu_ref, xs_ref, wg_ref, wu_ref, wd_ref, y_o, wg_b, wu_b, wd_b):
    i = pl.program_id(0)

    @pl.when(i < nu_ref[0])
    def _():
        @pl.when((i == 0) | (be_ref[i] != be_ref[jnp.maximum(i - 1, 0)]))
        def _():
            wg_b[...] = wg_ref[...].astype(BF16)
            wu_b[...] = wu_ref[...].astype(BF16)
            wd_b[...] = wd_ref[...].astype(BF16)

        xb = xs_ref[...].astype(BF16)
        gate = _dot(xb, wg_b[...])
        h = gate * _sigmoid(gate) * _dot(xb, wu_b[...])
        y_o[...] = _dot(h.astype(BF16), wd_b[...])

    @pl.when(i >= nu_ref[0])
    def _():
        y_o[...] = jnp.zeros_like(y_o)


def _experts(block_e, n_used, xs, w_gate, w_up, w_down, layer):
    n_blocks = xs.shape[0] // SLOT_BLOCK
    blk = pl.BlockSpec((SLOT_BLOCK, D_MODEL), lambda i, be, nu: (i, 0))
    wsel = lambda i, be, nu: (layer, be[i], 0, 0)
    grid_spec = pltpu.PrefetchScalarGridSpec(
        num_scalar_prefetch=2,
        grid=(n_blocks,),
        in_specs=[blk,
                  pl.BlockSpec((None, None, D_MODEL, D_EXPERT), wsel),
                  pl.BlockSpec((None, None, D_MODEL, D_EXPERT), wsel),
                  pl.BlockSpec((None, None, D_EXPERT, D_MODEL), wsel)],
        out_specs=blk,
        scratch_shapes=[pltpu.VMEM((D_MODEL, D_EXPERT), BF16), pltpu.VMEM((D_MODEL, D_EXPERT), BF16),
                        pltpu.VMEM((D_EXPERT, D_MODEL), BF16)])
    return pl.pallas_call(
        _expert_kernel,
        grid_spec=grid_spec,
        out_shape=jax.ShapeDtypeStruct(xs.shape, F32),
        compiler_params=_params("arbitrary"),
        name="experts",
    )(block_e, n_used, xs, w_gate, w_up, w_down)


def _combine_kernel(dest_ref, x_ref, p_ref, wt_ref, wpg, wple, g_ref, b_ref, yb_ref, x2_o, buf, sem, *, tm, n):
    i = pl.program_id(0)

    def row_copy(r, k):
        return pltpu.make_async_copy(yb_ref.at[pl.ds(dest_ref[k * n + i * tm + r], 1)],
                                     buf.at[k, pl.ds(r, 1)], sem.at[k])

    def issue(r, c):
        row_copy(r, 0).start()
        row_copy(r, 1).start()
        return c

    def drain(r, c):
        row_copy(r, 0).wait()
        row_copy(r, 1).wait()
        return c

    for r in range(tm):
        issue(r, 0)
    x = x_ref[...]
    ple = _sigmoid(_dot(x.astype(BF16), wpg[...])) * _dot(p_ref[...].astype(BF16), wple[...])
    lax.fori_loop(0, tm, drain, 0, unroll=DMA_UNROLL)
    wt = wt_ref[...]
    y = wt[:, 0:1] * buf[0] + wt[:, 1:2] * buf[1]
    x2_o[...] = _layer_norm(DEEPNORM_ALPHA * x + y + ple, g_ref[...], b_ref[...])


def _combine(dest, x1, p, wt, wpg, wple, g, b, yb, tm):
    n = x1.shape[0]
    row = lambda i, d: (i, 0)
    grid_spec = pltpu.PrefetchScalarGridSpec(
        num_scalar_prefetch=1,
        grid=(n // tm,),
        in_specs=[pl.BlockSpec((tm, D_MODEL), row), pl.BlockSpec((tm, PLE_DIM), row),
                  pl.BlockSpec((tm, SUBLANES), row)]
        + [pl.BlockSpec(a.shape, lambda i, d: (0, 0)) for a in (wpg, wple, g, b)]
        + [pl.BlockSpec(memory_space=pl.ANY)],
        out_specs=pl.BlockSpec((tm, D_MODEL), row),
        scratch_shapes=[pltpu.VMEM((2, tm, D_MODEL), F32), pltpu.SemaphoreType.DMA((2,))])
    return pl.pallas_call(
        functools.partial(_combine_kernel, tm=tm, n=n),
        grid_spec=grid_spec,
        out_shape=jax.ShapeDtypeStruct((n, D_MODEL), F32),
        compiler_params=_params("arbitrary"),
        name="combine",
    )(dest, x1, p, wt, wpg, wple, g, b, yb)


def _tile(n, pref):
    return pref if n % pref == 0 else n


def _rope_tables(pos):
    def tab(half, reps):
        freq = ROPE_THETA ** (-jnp.arange(half, dtype=F32) / half)
        ang = pos.astype(F32)[:, None] * freq[None, :]
        cos, sin = jnp.cos(ang), jnp.sin(ang)
        return jnp.tile(jnp.concatenate([cos, cos], -1), (1, reps)), jnp.tile(jnp.concatenate([-sin, sin], -1), (1, reps))
    return tab(HEAD_DIM // 2, 1) + tab(IDX_DIM // 2, 2)


def _split_w_in(w_in):
    sizes = (N_HEADS * HEAD_DIM, N_KV_HEADS * HEAD_DIM, N_KV_HEADS * HEAD_DIM, N_IDX_HEADS * IDX_DIM,
             N_IDX_HEADS, IDX_DIM, D_MODEL, D_MODEL, D_MODEL, 3 * D_MODEL)
    parts, o = [], 0
    for s in sizes:
        parts.append(w_in[:, o:o + s].astype(BF16))
        o += s
    wq, wk, wv, wiq, wiw, wik, wxr, wgr, wxq, wgz = parts
    wiw = jnp.pad(wiw, ((0, 0), (0, LANES - N_IDX_HEADS)))
    wik = jnp.concatenate([wik, wik], axis=1)
    return (wq, wk, wv, wiq, wiw, wik), (wxr, wgr, wxq), wgz


def _ffn(x1, p, lw, xs_init=None):
    n = x1.shape[0]
    tm = _tile(n, 512)
    ei, wt, counts = _router(x1, lw["w_rt"], lw["b_rt"], tm)
    counts = counts[:, 0].astype(I32)
    padded = (counts + SLOT_BLOCK - 1) // SLOT_BLOCK * SLOT_BLOCK
    pad_end = jnp.cumsum(padded)
    pad_start = pad_end - padded
    experts = jnp.arange(N_EXPERTS, dtype=I32)[:, None, None]
    start = jnp.sum(jnp.where(ei[None, 0:2] == experts, pad_start[:, None, None], 0), axis=0)
    dest = (start + ei[2:4]).reshape(-1)
    n_blocks = -(-2 * n // SLOT_BLOCK) + N_EXPERTS
    blk_start = jnp.arange(n_blocks, dtype=I32) * SLOT_BLOCK
    block_e = jnp.minimum(jnp.sum((pad_end[None, :] <= blk_start[:, None]).astype(I32), axis=1), N_EXPERTS - 1)
    n_used = (pad_end[-1:] // SLOT_BLOCK).astype(I32)
    xs = _dispatch(dest, x1, n_blocks * SLOT_BLOCK, tm, xs_init)
    yb = _experts(block_e, n_used, xs, lw["w_gate"], lw["w_up"], lw["w_down"], lw["layer"])
    tc = _tile(n, 256)
    return _combine(dest, x1, p, wt.T, lw["w_ple_gate"], lw["w_ple"], lw["ln2_g"], lw["ln2_b"], yb, tc), xs


def kernel(x_prompt, x_sample, p_prompt, p_sample, cache_k, cache_v, cache_idx, state_lru_h, state_lru_conv, state_pool, page_table, w_in, w_out, lru_conv_w, lru_conv_b, lru_wa, lru_ba, lru_wi, lru_bi, lru_lambda, pool_w, pool_scale, ln1_g, ln1_b, w_router_group, b_router_group, w_router_expert, b_router_expert, w_exp_gate, w_exp_up, w_exp_down, w_ple, w_ple_gate, ln2_g, ln2_b):
    bp, tp = x_prompt.shape[:2]
    bs, ts = x_sample.shape[:2]
    assert ts == 1, "the sample group decodes one token per sequence"
    depth = w_in.shape[0]
    past = page_table.shape[1] * PAGE_SIZE
    n_p = bp * tp
    tm_p = _tile(tp, 512)
    tq = _tile(tp, DSA_Q_BLOCK)
    y_att_p = slots_p = slots_s = None
    tabs_p = _rope_tables(jnp.arange(tp))
    tabs_s = _rope_tables(jnp.full((bs,), past, I32))
    cache_idx_t = jnp.swapaxes(cache_idx, 2, 3)

    xp = x_prompt.reshape(n_p, D_MODEL)
    xs = x_sample.reshape(bs, D_MODEL)
    outs = [[] for _ in range(12)]
    row2 = lambda a: a.reshape(1, -1)
    for i in range(depth):
        w_attn, w_mix, wgz = _split_w_in(w_in[i])
        mix_w = w_mix + (lru_conv_w[i], row2(lru_conv_b[i]), lru_wa[i].astype(BF16), row2(lru_ba[i]),
                         lru_wi[i].astype(BF16), row2(lru_bi[i]), row2(lru_lambda[i]),
                         pool_w[i].astype(BF16), row2(pool_scale[i]))
        wout = w_out[i].astype(BF16)
        g1, b1 = row2(ln1_g[i]), row2(ln1_b[i])
        w_rt = jnp.concatenate([w_router_expert[i].T, w_router_group[i].T,
                                jnp.zeros((ROUTER_ROWS - N_EXPERTS - N_EXPERT_GROUPS, D_MODEL), F32)], 0)
        b_rt = jnp.concatenate([b_router_expert[i], b_router_group[i],
                                jnp.zeros((ROUTER_ROWS - N_EXPERTS - N_EXPERT_GROUPS,), F32)])[:, None]
        lw = dict(w_rt=w_rt, b_rt=b_rt, w_gate=w_exp_gate, w_up=w_exp_up, w_down=w_exp_down, layer=i,
                  w_ple_gate=w_ple_gate[i].astype(BF16),
                  w_ple=w_ple[i].astype(BF16), ln2_g=row2(ln2_g[i]), ln2_b=row2(ln2_b[i]))

        q, k, v, kb, vb, iq, iw, ik = _proj_attn(xp, w_attn, tabs_p, tm_p, tp // tm_p)
        y_att_p = _dsa_prompt(q, kb, vb, iq, iw, ik, bp, tp, tq, y_att_p)
        y_lru, y_pool, h_new, conv_new, pool_new = _mix_prompt(xp.reshape(bp, tp, D_MODEL), mix_w, bp, tp)
        x1 = _merge(xp, y_att_p, y_lru.reshape(n_p, D_MODEL), y_pool.reshape(n_p, D_MODEL), wgz, wout, g1, b1,
                    _tile(n_p, 256))
        xp, slots_p = _ffn(x1, p_prompt[i].reshape(n_p, PLE_DIM), lw, slots_p)
        outs[0].append(k.reshape(bp, tp, N_KV_HEADS, HEAD_DIM))
        outs[1].append(v.reshape(bp, tp, N_KV_HEADS, HEAD_DIM))
        outs[2].append(ik[:, :IDX_DIM].reshape(bp, tp, IDX_DIM))
        outs[3].append(h_new)
        outs[4].append(conv_new[:, SUBLANES - (CONV_WIDTH - 1):])
        outs[5].append(pool_new[:, 2 * SUBLANES - POOL_BUF:])

        q, k, v, _, _, iq, iw, ik = _proj_attn(xs, w_attn, tabs_s, bs, 1)
        sel_idx = _dsa_sample_select(page_table, iq, iw, ik[:, :IDX_DIM], cache_idx_t, i)
        y_att = _dsa_sample_attend(sel_idx[:, :, 0], page_table, q, k, v, cache_k, cache_v, i)
        y_lru, y_pool, h_new, xr, xq = _mix_sample(xs, mix_w, state_lru_conv[i], state_lru_h[i], state_pool[i], past)
        x1 = _merge(xs, y_att, y_lru, y_pool, wgz, wout, g1, b1, bs)
        xs, slots_s = _ffn(x1, p_sample[i].reshape(bs, PLE_DIM), lw, slots_s)
        outs[6].append(k.reshape(bs, ts, N_KV_HEADS, HEAD_DIM))
        outs[7].append(v.reshape(bs, ts, N_KV_HEADS, HEAD_DIM))
        outs[8].append(ik[:, :IDX_DIM].reshape(bs, ts, IDX_DIM))
        outs[9].append(h_new)
        outs[10].append(jnp.concatenate([state_lru_conv[i][:, 1:], xr[:, None]], 1))
        outs[11].append(jnp.concatenate([state_pool[i][:, 1:], xq[:, None]], 1))

    return (xp.reshape(bp, tp, D_MODEL), xs.reshape(bs, ts, D_MODEL)) + tuple(jnp.stack(o) for o in outs)
```

```python
import functools

import jax
import jax.numpy as jnp
import numpy as np
from jax import lax
from jax.experimental import pallas as pl
from jax.experimental.pallas import tpu as pltpu

F32 = jnp.float32
BF16 = jnp.bfloat16
I32 = jnp.int32

D_MODEL = 1024
N_HEADS = 8
HEAD_DIM = 128
N_KV_HEADS = 4
KV_GROUP = N_HEADS // N_KV_HEADS
N_IDX_HEADS = 8
IDX_DIM = 64
INDEX_SCALE = (IDX_DIM * N_IDX_HEADS) ** -0.5
TOPK_MAX = 256
ROPE_THETA = 10000.0
PAGE_SIZE = 128
N_LRU_BLOCKS = 8
LRU_BLOCK = D_MODEL // N_LRU_BLOCKS
CONV_WIDTH = 4
LRU_C = 8.0
POOL_WINDOWS = (2, 4, 8, 16)
POOL_GROUP = D_MODEL // len(POOL_WINDOWS)
POOL_BUF = max(POOL_WINDOWS) - 1
N_EXPERT_GROUPS = 4
EXPERTS_PER_GROUP = 8
N_EXPERTS = N_EXPERT_GROUPS * EXPERTS_PER_GROUP
D_EXPERT = 512
PLE_DIM = 256
LN_EPS = 1e-5
DEPTH = 2
DEEPNORM_ALPHA = (2 * DEPTH) ** 0.25
ATTN_SCALE = HEAD_DIM ** -0.5

LANES = 128
SUBLANES = 8
SLOT_BLOCK = 512
DSA_EXTENTS = 8
DSA_Q_BLOCK = 256
RADIX4_MAX_ELEMS = 256 * 1024
SEQ_GROUP = 8
MIX_SEQS = 8
MIX_STEPS = 64
DMA_UNROLL = 8
VMEM_LIMIT = 48 * 1024 * 1024
INT_MIN = -2 ** 31
KEY_NEG_INF = INT_MIN + 0x7FFFFF

_NT = (((1,), (1,)), ((), ()))


def _dot(a, b):
    return jnp.dot(a, b, preferred_element_type=F32)


def _dot_nt(a, b):
    return lax.dot_general(a, b, _NT, preferred_element_type=F32)


def _sigmoid(x):
    return 1.0 / (1.0 + jnp.exp(-x))


def _gelu_tanh(x):
    c = np.float32(np.sqrt(2.0 / np.pi))
    return x * (0.5 * (1.0 + jnp.tanh(c * (x + 0.044715 * (x * x * x)))))


def _softplus(x):
    return jnp.maximum(x, 0.0) + jnp.log1p(jnp.exp(-jnp.abs(x)))


def _layer_norm(r, g, b):
    mu = jnp.mean(r, axis=-1, keepdims=True)
    c = r - mu
    var = jnp.mean(c * c, axis=-1, keepdims=True)
    return c * lax.rsqrt(var + LN_EPS) * g + b


def _sort_key(x):
    bits = pltpu.bitcast(jnp.where(x == 0.0, 0.0, x), I32)
    return jnp.where(bits >= 0, bits, bits ^ 0x7FFFFFFF)


def _params(*sem):
    return pltpu.CompilerParams(dimension_semantics=sem, vmem_limit_bytes=VMEM_LIMIT)


def _full(shape):
    n = len(shape)
    return pl.BlockSpec(shape, lambda *_: (0,) * n)


def _proj_attn_kernel(x_ref, wq, wk, wv, wiq, wiw, wik, cos_ref, sin_ref, cosi_ref, sini_ref,
                      q_o, k_o, v_o, kb_o, vb_o, iq_o, iw_o, ik_o):
    xb = x_ref[...].astype(BF16)
    cos, sin = cos_ref[...], sin_ref[...]
    cosi, sini = cosi_ref[...], sini_ref[...]
    lane = lax.broadcasted_iota(I32, cos.shape, 1)
    low_half = (lane & (IDX_DIM - 1)) < (IDX_DIM // 2)

    def rope128(z):
        return z * cos + pltpu.roll(z, HEAD_DIM // 2, 1) * sin

    def rope64(z):
        partner = jnp.where(low_half, pltpu.roll(z, LANES - IDX_DIM // 2, 1), pltpu.roll(z, IDX_DIM // 2, 1))
        return z * cosi + partner * sini

    q = _dot(xb, wq[...])
    for h in range(N_HEADS):
        sl = slice(h * LANES, (h + 1) * LANES)
        q_o[:, sl] = (rope128(q[:, sl]) * ATTN_SCALE).astype(BF16)
    k = _dot(xb, wk[...])
    for c in range(N_KV_HEADS):
        sl = slice(c * LANES, (c + 1) * LANES)
        kr = rope128(k[:, sl])
        k_o[:, sl] = kr
        kb_o[:, sl] = kr.astype(BF16)
    v = _dot(xb, wv[...])
    v_o[...] = v
    vb_o[...] = v.astype(BF16)
    iq = _dot(xb, wiq[...])
    for j in range(N_IDX_HEADS * IDX_DIM // LANES):
        sl = slice(j * LANES, (j + 1) * LANES)
        iq_o[:, sl] = rope64(iq[:, sl])
    iw_o[...] = _dot(xb, wiw[...])
    ik_o[...] = rope64(_dot(xb, wik[...]))


def _proj_attn(x, w, tabs, tm, n_tab_blocks):
    n = x.shape[0]
    cos, sin, cosi, sini = tabs
    row = lambda i: (i, 0)
    tab = lambda i: (i % n_tab_blocks, 0)
    wspec = lambda a: _full(a.shape)
    outs = [
        ((n, 1024), BF16), ((n, 512), F32), ((n, 512), F32), ((n, 512), BF16), ((n, 512), BF16),
        ((n, 512), F32), ((n, LANES), F32), ((n, LANES), F32)]
    return pl.pallas_call(
        _proj_attn_kernel,
        grid=(n // tm,),
        in_specs=[pl.BlockSpec((tm, D_MODEL), row)] + [wspec(a) for a in w]
        + [pl.BlockSpec((tm, LANES), tab)] * 4,
        out_specs=[pl.BlockSpec((tm, s[1]), row) for s, _ in outs],
        out_shape=[jax.ShapeDtypeStruct(s, d) for s, d in outs],
        compiler_params=_params("parallel"),
        name="proj_attn",
    )(x, *w, cos, sin, cosi, sini)


def _count_rows(mask):
    return jnp.sum(jnp.where(mask, 1.0, 0.0), axis=1, keepdims=True)


def _dsa_prompt_kernel(q_ref, kb_ref, vb_ref, iq_ref, iw_ref, ik_ref, *rest, tq, t_len, q0, topk, idx_bits):
    o_ref, p_sc = rest[-2:]
    qi = q0 + pl.program_id(1)
    ikb = ik_ref[...].astype(BF16)
    iq = iq_ref[...]
    iw = iw_ref[...] * INDEX_SCALE
    lane = lax.broadcasted_iota(I32, (tq, LANES), 1)
    score = jnp.zeros((tq, t_len), F32)

    for h in range(N_IDX_HEADS):
        chunk = iq[:, (h // 2) * LANES:(h // 2 + 1) * LANES]
        keep = (lane < IDX_DIM) if h % 2 == 0 else (lane >= IDX_DIM)
        s = _dot_nt(jnp.where(keep, chunk, 0.0).astype(BF16), ikb)
        score = score + jnp.maximum(s, 0.0) * iw[:, h:h + 1]
    qpos = qi * tq + lax.broadcasted_iota(I32, (tq, 1), 0)
    kpos = lax.broadcasted_iota(I32, (tq, t_len), 1)
    causal = kpos <= qpos
    key = _sort_key(jnp.where(causal, score, -jnp.inf))

    if tq * t_len <= RADIX4_MAX_ELEMS:
        def thr_body(b, t):
            step = lax.shift_left(jnp.int32(1), 30 - 2 * b)
            cands = [t + step, t + 2 * step, t + 3 * step]
            ok = [_count_rows(key >= c) >= topk for c in cands]
            return jnp.where(ok[2], cands[2], jnp.where(ok[1], cands[1], jnp.where(ok[0], cands[0], t)))
        n_steps = 16
    else:
        def thr_body(b, t):
            cand = t + lax.shift_left(jnp.int32(1), 31 - b)
            return jnp.where(_count_rows(key >= cand) >= topk, cand, t)
        n_steps = 32

    thr = lax.fori_loop(0, n_steps, thr_body, jnp.full((tq, 1), INT_MIN, I32))
    gt = key > thr
    eq = key == thr
    need = topk - _count_rows(gt)
    tie = (_count_rows(eq) > need) & (thr > KEY_NEG_INF)
    p_sc[...] = jnp.full((tq, 1), t_len, I32)

    @pl.when(jnp.max(jnp.where(tie, 1.0, 0.0)) > 0.0)
    def _():
        def pos_body(b, p):
            cand = p + lax.shift_left(jnp.int32(1), idx_bits - 1 - b)
            return jnp.where(_count_rows(eq & (kpos < cand)) < need, cand, p)
        p_sc[...] = lax.fori_loop(0, idx_bits, pos_body, jnp.zeros((tq, 1), I32))

    sel = causal & (gt | (eq & (kpos <= p_sc[...])))
    bias = jnp.where(sel, 0.0, -jnp.inf)
    for c in range(N_KV_HEADS):
        kc = kb_ref[:, c * LANES:(c + 1) * LANES]
        vc = vb_ref[:, c * LANES:(c + 1) * LANES]
        vx = jnp.concatenate([vc, jnp.ones_like(vc)], axis=1)
        for g in range(KV_GROUP):
            h = c * KV_GROUP + g
            s = _dot_nt(q_ref[:, h * LANES:(h + 1) * LANES], kc) + bias
            m = jnp.max(s, axis=1, keepdims=True)
            pv = _dot(jnp.exp(s - m).astype(BF16), vx)
            o_ref[:, h * LANES:(h + 1) * LANES] = pv[:, 0:LANES] / pv[:, LANES:LANES + 1]


def _dsa_prompt(q, kb, vb, iq, iw, ik, batch, t_len, tq, y_init=None):
    topk = min(TOPK_MAX, t_len // 4)
    nq = t_len // tq
    n_var = max(d for d in range(1, DSA_EXTENTS + 1) if nq % d == 0)
    per = nq // n_var
    kb, vb, ik = (a.reshape(batch, t_len, a.shape[-1]) for a in (kb, vb, ik))
    y = jnp.zeros((batch * t_len, 1024), F32) if y_init is None else y_init
    for v in range(n_var):
        ext = (v + 1) * per * tq
        qrow = lambda b, i, v=v: (b * nq + v * per + i, 0)
        brow = lambda b, i: (b, 0, 0)
        in_specs = [pl.BlockSpec((tq, 1024), qrow), pl.BlockSpec((None, ext, 512), brow),
                    pl.BlockSpec((None, ext, 512), brow), pl.BlockSpec((tq, 512), qrow),
                    pl.BlockSpec((tq, LANES), qrow), pl.BlockSpec((None, ext, LANES), brow)]
        in_specs.append(pl.BlockSpec(memory_space=pl.ANY))
        y = pl.pallas_call(
            functools.partial(_dsa_prompt_kernel, tq=tq, t_len=ext, q0=v * per, topk=topk,
                              idx_bits=max(1, int(np.ceil(np.log2(ext))))),
            grid=(batch, per),
            in_specs=in_specs,
            out_specs=pl.BlockSpec((tq, 1024), qrow),
            out_shape=jax.ShapeDtypeStruct((batch * t_len, 1024), F32),
            scratch_shapes=[pltpu.VMEM((tq, 1), I32)],
            input_output_aliases={6: 0},
            compiler_params=_params("parallel", "arbitrary"),
            name="dsa_prompt",
        )(q, kb, vb, iq, iw, ik, y)
    return y


def _dsa_sample_select_kernel(pt_ref, iq_ref, iw_ref, ikn_ref, cidx_ref, idx_o,
                              ikbuf, s_sc, sem, *, layer, n_pages, topk, idx_bits, group):
    g0 = pl.program_id(0) * group
    past = n_pages * PAGE_SIZE

    def page_copy(s, j):
        slot = s % 2
        return pltpu.make_async_copy(cidx_ref.at[layer, pt_ref[(g0 + s) * n_pages + j]],
                                     ikbuf.at[slot, j], sem.at[slot])

    def issue_seq(s):
        def body(j, c):
            page_copy(s, j).start()
            return c
        lax.fori_loop(0, n_pages, body, 0, unroll=DMA_UNROLL)

    def drain_seq(s):
        def body(j, c):
            page_copy(s, j).wait()
            return c
        lax.fori_loop(0, n_pages, body, 0, unroll=DMA_UNROLL)

    issue_seq(0)

    def seq_body(s, c):
        @pl.when(s + 1 < group)
        def _():
            issue_seq(s + 1)
        drain_seq(s)
        slot = s % 2
        iq8 = iq_ref[s].astype(BF16)
        w8 = iw_ref[s] * INDEX_SCALE
        for j in range(n_pages):
            s8 = _dot(iq8, ikbuf[slot, j].astype(BF16))
            s_sc[s, j:j + 1, :] = jnp.sum(jnp.maximum(s8, 0.0) * w8, axis=0, keepdims=True)
        return c

    lax.fori_loop(0, group, seq_body, 0)

    w_all = iw_ref[...] * INDEX_SCALE
    own = jnp.sum(iq_ref[...].astype(BF16).astype(F32) * ikn_ref[...].astype(BF16).astype(F32),
                  axis=2, keepdims=True)
    own = jnp.sum(jnp.maximum(own, 0.0) * w_all, axis=1, keepdims=True)

    key = _sort_key(s_sc[...])
    key_own = _sort_key(own)
    pos = (lax.broadcasted_iota(I32, key.shape, 1) * PAGE_SIZE + lax.broadcasted_iota(I32, key.shape, 2))

    def count(mask, mask_own):
        c = jnp.sum(jnp.where(mask, 1.0, 0.0), axis=1, keepdims=True)
        return jnp.sum(c, axis=2, keepdims=True) + jnp.where(mask_own, 1.0, 0.0)

    def thr_body(i, t):
        cand = t + lax.shift_left(jnp.int32(1), 31 - i)
        return jnp.where(count(key >= cand, key_own >= cand) >= topk, cand, t)

    thr = lax.fori_loop(0, 32, thr_body, jnp.full((group, 1, 1), INT_MIN, I32))
    gt, eq = key > thr, key == thr
    gt_own, eq_own = key_own > thr, key_own == thr
    need = topk - count(gt, gt_own)

    def pos_body(i, p):
        cand = p + lax.shift_left(jnp.int32(1), idx_bits - 1 - i)
        return jnp.where(count(eq & (pos < cand), eq_own & (past < cand)) < need, cand, p)

    plast = lax.fori_loop(0, idx_bits, pos_body, jnp.zeros((group, 1, 1), I32))
    s_sc[...] = jnp.where(gt | (eq & (pos <= plast)), 1.0, 0.0)

    r_i = lax.broadcasted_iota(I32, (PAGE_SIZE, PAGE_SIZE), 0)
    c_i = lax.broadcasted_iota(I32, (PAGE_SIZE, PAGE_SIZE), 1)
    tri_incl = jnp.where(r_i <= c_i, 1.0, 0.0).astype(BF16)
    pr = lax.broadcasted_iota(I32, (n_pages, n_pages), 0)
    pc = lax.broadcasted_iota(I32, (n_pages, n_pages), 1)
    tri_pages = jnp.where(pr <= pc, 1.0, 0.0).astype(BF16)
    ones_rows = jnp.ones((SUBLANES, PAGE_SIZE), BF16)
    rank = lax.broadcasted_iota(I32, (topk, 1), 0).astype(F32)
    page_lane = lax.broadcasted_iota(I32, (topk, n_pages), 1).astype(F32)

    def compact(s, c):
        selb = s_sc[s].astype(BF16)
        within = _dot(selb, tri_incl)
        n_row = _dot_nt(ones_rows, selb)[0:1]
        end_row = _dot(jnp.broadcast_to(n_row, (SUBLANES, n_pages)).astype(BF16), tri_pages)[0:1]
        before = end_row <= rank
        page_of = jnp.sum(jnp.where(before, 1.0, 0.0), axis=1, keepdims=True)
        local = rank - jnp.sum(jnp.where(before, n_row, 0.0), axis=1, keepdims=True)
        page_cum = _dot(jnp.where(page_lane == page_of, 1.0, 0.0).astype(BF16), within.astype(BF16))
        lane_of = jnp.sum(jnp.where(page_cum <= local, 1.0, 0.0), axis=1, keepdims=True)
        idx_o[s] = jnp.minimum(page_of * PAGE_SIZE + lane_of, float(past)).astype(I32)
        return c

    lax.fori_loop(0, group, compact, 0)


def _dsa_sample_select(page_table, iq, iw, ik_new, cache_idx_t, layer):
    bd, n_pages = page_table.shape
    past = n_pages * PAGE_SIZE
    topk = min(TOPK_MAX, (past + 1) // 4)
    idx_bits = int(np.floor(np.log2(past))) + 1
    group = SEQ_GROUP if bd % SEQ_GROUP == 0 else bd
    grp = lambda g, pt: (g, 0, 0)
    grid_spec = pltpu.PrefetchScalarGridSpec(
        num_scalar_prefetch=1,
        grid=(bd // group,),
        in_specs=[pl.BlockSpec((group, N_IDX_HEADS, IDX_DIM), grp),
                  pl.BlockSpec((group, N_IDX_HEADS, 1), grp),
                  pl.BlockSpec((group, 1, IDX_DIM), grp),
                  pl.BlockSpec(memory_space=pl.ANY)],
        out_specs=pl.BlockSpec((group, topk, 1), grp),
        scratch_shapes=[pltpu.VMEM((2, n_pages, IDX_DIM, PAGE_SIZE), F32), pltpu.VMEM((group, n_pages, PAGE_SIZE), F32),
                        pltpu.SemaphoreType.DMA((2,))])
    return pl.pallas_call(
        functools.partial(_dsa_sample_select_kernel, layer=layer, n_pages=n_pages, topk=topk, idx_bits=idx_bits,
                          group=group),
        grid_spec=grid_spec,
        out_shape=jax.ShapeDtypeStruct((bd, topk, 1), I32),
        compiler_params=_params("arbitrary"),
        name="dsa_sample_select",
    )(page_table.reshape(-1), iq.reshape(bd, N_IDX_HEADS, IDX_DIM), iw[:, :N_IDX_HEADS, None],
      ik_new[:, None, :], cache_idx_t)


def _dsa_sample_attend_kernel(idx_ref, pt_ref, q_ref, kn_ref, vn_ref, idxv_ref, ck_ref, cv_ref, o_ref,
                              kbuf0, vbuf0, kbuf1, vbuf1, sem, *, layer, n_pages, topk):
    b = pl.program_id(0)
    last = pl.num_programs(0) - 1
    past = n_pages * PAGE_SIZE
    bufs = ((kbuf0, vbuf0), (kbuf1, vbuf1))

    def row_copies(seq, buf, r):
        pidx = jnp.minimum(idx_ref[seq * topk + r], past - 1)
        phys = pt_ref[seq * n_pages + pidx // PAGE_SIZE]
        off = pidx % PAGE_SIZE
        return (pltpu.make_async_copy(ck_ref.at[layer, phys, off], bufs[buf][0].at[r], sem.at[buf, 0]),
                pltpu.make_async_copy(cv_ref.at[layer, phys, off], bufs[buf][1].at[r], sem.at[buf, 1]))

    def issue_all(seq, buf):
        for r in range(topk):
            for cp in row_copies(seq, buf, r):
                cp.start()

    def drain_all(seq, buf):
        def body(r, c):
            for cp in row_copies(seq, buf, r):
                cp.wait()
            return c
        lax.fori_loop(0, topk, body, 0, unroll=DMA_UNROLL)

    @pl.when(b == 0)
    def _():
        issue_all(0, 0)

    def step(buf):
        drain_all(b, buf)
        nxt = jnp.minimum(b + 1, last)
        issue_all(nxt, 1 - buf)
        own = idxv_ref[...] >= past
        k_sel = jnp.where(own, kn_ref[...][None], bufs[buf][0][...])
        v_sel = jnp.where(own, vn_ref[...][None], bufs[buf][1][...])
        for g in range(KV_GROUP):
            s = jnp.sum(k_sel * q_ref[g][None], axis=-1, keepdims=True)
            m = jnp.max(s, axis=0, keepdims=True)
            p = jnp.exp(s - m)
            l = jnp.sum(p, axis=0)
            o_ref[g] = jnp.sum(p * v_sel, axis=0) / l

        @pl.when(b == last)
        def _():
            drain_all(nxt, 1 - buf)

    for parity in range(2):
        pl.when(b % 2 == parity)(functools.partial(step, parity))


def _dsa_sample_attend(idx, page_table, q, k_new, v_new, cache_k, cache_v, layer):
    bd, n_pages = page_table.shape
    topk = idx.shape[1]
    qg = q.astype(F32).reshape(bd, N_KV_HEADS, KV_GROUP, HEAD_DIM).transpose(0, 2, 1, 3)
    head = lambda b, *_: (b, 0, 0)
    grid_spec = pltpu.PrefetchScalarGridSpec(
        num_scalar_prefetch=2,
        grid=(bd,),
        in_specs=[pl.BlockSpec((None, KV_GROUP, N_KV_HEADS, HEAD_DIM), lambda b, *_: (b, 0, 0, 0)),
                  pl.BlockSpec((None, N_KV_HEADS, HEAD_DIM), head),
                  pl.BlockSpec((None, N_KV_HEADS, HEAD_DIM), head),
                  pl.BlockSpec((None, topk, 1, 1), lambda b, *_: (b, 0, 0, 0)),
                  pl.BlockSpec(memory_space=pl.ANY), pl.BlockSpec(memory_space=pl.ANY)],
        out_specs=pl.BlockSpec((None, KV_GROUP, N_KV_HEADS, HEAD_DIM), lambda b, *_: (b, 0, 0, 0)),
        scratch_shapes=[pltpu.VMEM((topk, N_KV_HEADS, HEAD_DIM), F32)] * 4 + [pltpu.SemaphoreType.DMA((2, 2))])
    o = pl.pallas_call(
        functools.partial(_dsa_sample_attend_kernel, layer=layer, n_pages=n_pages, topk=topk),
        grid_spec=grid_spec,
        out_shape=jax.ShapeDtypeStruct((bd, KV_GROUP, N_KV_HEADS, HEAD_DIM), F32),
        compiler_params=_params("arbitrary"),
        name="dsa_sample_attend",
    )(idx.reshape(-1), page_table.reshape(-1), qg, k_new.reshape(bd, N_KV_HEADS, HEAD_DIM),
      v_new.reshape(bd, N_KV_HEADS, HEAD_DIM), idx.reshape(bd, topk, 1, 1), cache_k, cache_v)
    return o.transpose(0, 2, 1, 3).reshape(bd, N_HEADS * HEAD_DIM)


def _lru_gates(xc, wa, ba, wi, bi, lam):
    xcb = xc.astype(BF16)
    a_parts, u_parts = [], []
    for n in range(N_LRU_BLOCKS):
        sl = slice(n * LRU_BLOCK, (n + 1) * LRU_BLOCK)
        r = _sigmoid(_dot(xcb[:, sl], wa[n]) + ba[:, sl])
        i = _sigmoid(_dot(xcb[:, sl], wi[n]) + bi[:, sl])
        log_a = -LRU_C * r * _softplus(-lam[:, sl])
        a_parts.append(jnp.exp(log_a))
        th = jnp.tanh(log_a)
        u_parts.append(jnp.sqrt(-2.0 * th / (1.0 - th)) * i * xc[:, sl])
    return a_parts, u_parts


def _pool_mix(window_sum, xq, cnt, pool_w, pool_scale, g):
    sl = slice(g * POOL_GROUP, (g + 1) * POOL_GROUP)
    pooled = window_sum / cnt - xq[:, sl]
    return _dot(pooled.astype(BF16), pool_w[g]) * pool_scale[:, sl]


def _mix_prompt_kernel(x_ref, wxr, wgr, wxq, convw, convb, wa, ba, wi, bi, lam, poolw, pscale,
                       ylru_o, ypool_o, h_o, conv_o, pool_o,
                       xr_ext, xq_ext, a_sc, u_sc, h_sc, h_carry, *, nb, tt):
    t = pl.program_id(1)
    halo_r, halo_q = SUBLANES, 2 * SUBLANES
    rows = nb * tt

    @pl.when(t == 0)
    def _():
        xr_ext[:, 0:halo_r] = jnp.zeros((nb, halo_r, D_MODEL), F32)
        xq_ext[:, 0:halo_q] = jnp.zeros((nb, halo_q, D_MODEL), F32)
        h_carry[...] = jnp.zeros_like(h_carry)

    @pl.when(t > 0)
    def _():
        xr_ext[:, 0:halo_r] = xr_ext[:, tt:tt + halo_r]
        xq_ext[:, 0:halo_q] = xq_ext[:, tt:tt + halo_q]

    xb = x_ref[...].reshape(rows, D_MODEL).astype(BF16)
    xr3 = _dot(xb, wxr[...]).reshape(nb, tt, D_MODEL)
    xq3 = _dot(xb, wxq[...]).reshape(nb, tt, D_MODEL)
    xr_ext[:, halo_r:halo_r + tt] = xr3
    xq_ext[:, halo_q:halo_q + tt] = xq3

    cw = convw[...]
    xc = convb[...] + cw[CONV_WIDTH - 1:CONV_WIDTH] * xr3
    for j in range(CONV_WIDTH - 1):
        o = halo_r - (CONV_WIDTH - 1) + j
        xc = xc + cw[j:j + 1] * xr_ext[:, o:o + tt]
    a_parts, u_parts = _lru_gates(xc.reshape(rows, D_MODEL), wa, ba[...], wi, bi[...], lam[...])
    for n in range(N_LRU_BLOCKS):
        sl = slice(n * LRU_BLOCK, (n + 1) * LRU_BLOCK)
        a_sc[:, sl] = a_parts[n]
        u_sc[:, sl] = u_parts[n]

    a3 = pltpu.einshape("btd->tbd", a_sc[...].reshape(nb, tt, D_MODEL))
    u3 = pltpu.einshape("btd->tbd", u_sc[...].reshape(nb, tt, D_MODEL))
    h = h_carry[...]
    for s in range(tt):
        h = a3[s] * h + u3[s]
        h_sc[s] = h
    h_carry[...] = h
    hs = pltpu.einshape("tbd->btd", h_sc[...]).reshape(rows, D_MODEL)
    ylru_o[...] = (hs * _gelu_tanh(_dot(xb, wgr[...]))).reshape(nb, tt, D_MODEL)

    posn = t * tt + lax.broadcasted_iota(I32, (1, tt, 1), 1)
    for g, w in enumerate(POOL_WINDOWS):
        sl = slice(g * POOL_GROUP, (g + 1) * POOL_GROUP)
        acc = xq_ext[:, :, sl]
        d = 1
        while d < w:
            acc = acc + pltpu.roll(acc, d, 1)
            d *= 2
        cnt = jnp.minimum(posn + 1, w).astype(F32)
        pooled = (acc[:, halo_q:] / cnt - xq3[:, :, sl]).reshape(rows, POOL_GROUP)
        mixed = _dot(pooled.astype(BF16), poolw[g]) * pscale[:, sl]
        ypool_o[:, :, sl] = mixed.reshape(nb, tt, POOL_GROUP)

    @pl.when(t == pl.num_programs(1) - 1)
    def _():
        h_o[...] = h
        conv_o[...] = xr_ext[:, tt:tt + halo_r]
        pool_o[...] = xq_ext[:, tt:tt + halo_q]


def _mix_prompt(x, w, batch, t_len):
    nb = _tile(batch, MIX_SEQS)
    tt = _tile(t_len, MIX_STEPS)
    blk = pl.BlockSpec((nb, tt, D_MODEL), lambda b, t: (b, t, 0))
    per_b = lambda b, t: (b, 0, 0)
    outs = [((batch, t_len, D_MODEL), F32), ((batch, t_len, D_MODEL), F32),
            ((batch, D_MODEL), F32), ((batch, SUBLANES, D_MODEL), F32), ((batch, 2 * SUBLANES, D_MODEL), F32)]
    return pl.pallas_call(
        functools.partial(_mix_prompt_kernel, nb=nb, tt=tt),
        grid=(batch // nb, t_len // tt),
        in_specs=[blk] + [_full(a.shape) for a in w],
        out_specs=[blk, blk, pl.BlockSpec((nb, D_MODEL), lambda b, t: (b, 0)),
                   pl.BlockSpec((nb, SUBLANES, D_MODEL), per_b), pl.BlockSpec((nb, 2 * SUBLANES, D_MODEL), per_b)],
        out_shape=[jax.ShapeDtypeStruct(s, d) for s, d in outs],
        scratch_shapes=[pltpu.VMEM((nb, tt + SUBLANES, D_MODEL), F32), pltpu.VMEM((nb, tt + 2 * SUBLANES, D_MODEL), F32),
                        pltpu.VMEM((nb * tt, D_MODEL), F32), pltpu.VMEM((nb * tt, D_MODEL), F32),
                        pltpu.VMEM((tt, nb, D_MODEL), F32), pltpu.VMEM((nb, D_MODEL), F32)],
        compiler_params=_params("parallel", "arbitrary"),
        name="mix_prompt",
    )(x, *w)


def _mix_sample_kernel(x_ref, wxr, wgr, wxq, convw, convb, wa, ba, wi, bi, lam, poolw, pscale,
                       conv_ref, h_ref, pool_ref, ylru_o, ypool_o, h_o, xr_o, xq_o, *, cnt_pos):
    xb = x_ref[...].astype(BF16)
    xr = _dot(xb, wxr[...])
    xq = _dot(xb, wxq[...])
    cw = convw[...]
    xc = convb[...] + cw[CONV_WIDTH - 1:CONV_WIDTH] * xr
    for j in range(CONV_WIDTH - 1):
        xc = xc + cw[j:j + 1] * conv_ref[j]
    a_parts, u_parts = _lru_gates(xc, wa, ba[...], wi, bi[...], lam[...])
    gate = _gelu_tanh(_dot(xb, wgr[...]))
    for n in range(N_LRU_BLOCKS):
        sl = slice(n * LRU_BLOCK, (n + 1) * LRU_BLOCK)
        h = a_parts[n] * h_ref[:, sl] + u_parts[n]
        h_o[:, sl] = h
        ylru_o[:, sl] = h * gate[:, sl]
    for g, w in enumerate(POOL_WINDOWS):
        sl = slice(g * POOL_GROUP, (g + 1) * POOL_GROUP)
        acc = xq[:, sl]
        for j in range(1, w):
            acc = acc + pool_ref[POOL_BUF - j, :, sl]
        ypool_o[:, sl] = _pool_mix(acc, xq, float(min(cnt_pos, w)), poolw, pscale[...], g)
    xr_o[...] = xr
    xq_o[...] = xq


def _mix_sample(x, w, conv_state, h_state, pool_state, past):
    bd = x.shape[0]
    args = (x, *w, conv_state.transpose(1, 0, 2), h_state, pool_state.transpose(1, 0, 2))
    shp = jax.ShapeDtypeStruct((bd, D_MODEL), F32)
    return pl.pallas_call(
        functools.partial(_mix_sample_kernel, cnt_pos=past + 1),
        in_specs=[_full(a.shape) for a in args],
        out_specs=[_full((bd, D_MODEL))] * 5,
        out_shape=[shp] * 5,
        grid=(1,),
        compiler_params=_params("arbitrary"),
        name="mix_sample",
    )(*args)


def _merge_kernel(x_ref, ya_ref, yl_ref, yp_ref, wgz, wout, g_ref, b_ref, x1_o):
    x = x_ref[...]
    gz = _dot(x.astype(BF16), wgz[...])
    merged = (_sigmoid(gz[:, 0:D_MODEL]) * ya_ref[...] + _sigmoid(gz[:, D_MODEL:2 * D_MODEL]) * yl_ref[...]
              + _sigmoid(gz[:, 2 * D_MODEL:3 * D_MODEL]) * yp_ref[...])
    r = DEEPNORM_ALPHA * x + _dot(merged.astype(BF16), wout[...])
    x1_o[...] = _layer_norm(r, g_ref[...], b_ref[...])


def _merge(x, ya, yl, yp, wgz, wout, g, b, tm):
    n = x.shape[0]
    row = pl.BlockSpec((tm, D_MODEL), lambda i: (i, 0))
    return pl.pallas_call(
        _merge_kernel,
        grid=(n // tm,),
        in_specs=[row] * 4 + [_full(wgz.shape), _full(wout.shape), _full(g.shape), _full(b.shape)],
        out_specs=row,
        out_shape=jax.ShapeDtypeStruct((n, D_MODEL), F32),
        compiler_params=_params("parallel"),
        name="merge",
    )(x, ya, yl, yp, wgz, wout, g, b)


ROUTER_ROWS = 40


def _first_argmax(v, n):
    m = jnp.max(v, axis=0, keepdims=True)
    rows = lax.broadcasted_iota(I32, v.shape, 0)
    return m, jnp.min(jnp.where(v == m, rows, n), axis=0, keepdims=True)


def _router_kernel(x_ref, w_ref, b_ref, ei_o, wt_o, cnt_o, carry, *, tm):
    i = pl.program_id(0)

    @pl.when(i == 0)
    def _():
        carry[...] = jnp.zeros_like(carry)

    def split(v):
        hi = v.astype(BF16)
        return hi, (v - hi.astype(F32)).astype(BF16)

    x_hi, x_lo = split(x_ref[...])
    w_hi, w_lo = split(w_ref[...])
    logits = _dot_nt(w_hi, x_hi) + (_dot_nt(w_hi, x_lo) + _dot_nt(w_lo, x_hi)) + b_ref[...]
    le = logits[0:N_EXPERTS]
    lg = logits[N_EXPERTS:N_EXPERTS + N_EXPERT_GROUPS]

    gmax, gidx = _first_argmax(lg, N_EXPERT_GROUPS)
    p_top = 1.0 / jnp.sum(jnp.exp(lg - gmax), axis=0, keepdims=True)
    le_g = jnp.zeros((EXPERTS_PER_GROUP, tm), F32)
    for gi in range(N_EXPERT_GROUPS):
        le_g = le_g + jnp.where(gidx == gi, le[gi * EXPERTS_PER_GROUP:(gi + 1) * EXPERTS_PER_GROUP], 0.0)
    m1, i1 = _first_argmax(le_g, EXPERTS_PER_GROUP)
    rows8 = lax.broadcasted_iota(I32, le_g.shape, 0)
    m2, i2 = _first_argmax(jnp.where(rows8 == i1, -jnp.inf, le_g), EXPERTS_PER_GROUP)
    z = jnp.sum(jnp.exp(le_g - m1), axis=0, keepdims=True)
    p1 = 1.0 / z
    p2 = jnp.exp(m2 - m1) / z
    e1 = gidx * EXPERTS_PER_GROUP + i1
    e2 = gidx * EXPERTS_PER_GROUP + i2

    rows = lax.broadcasted_iota(I32, (N_EXPERTS, tm), 0)
    hit1, hit2 = rows == e1, rows == e2
    onehot = jnp.where(hit1 | hit2, 1.0, 0.0)
    r_i = lax.broadcasted_iota(I32, (tm, tm), 0)
    c_i = lax.broadcasted_iota(I32, (tm, tm), 1)
    before = _dot(onehot.astype(BF16), jnp.where(r_i < c_i, 1.0, 0.0).astype(BF16)) + carry[...]
    carry[...] = carry[...] + jnp.sum(onehot, axis=1, keepdims=True)

    ei_o[0:1, :] = e1
    ei_o[1:2, :] = e2
    ei_o[2:3, :] = jnp.sum(jnp.where(hit1, before, 0.0), axis=0, keepdims=True).astype(I32)
    ei_o[3:4, :] = jnp.sum(jnp.where(hit2, before, 0.0), axis=0, keepdims=True).astype(I32)
    ei_o[4:SUBLANES, :] = jnp.zeros((SUBLANES - 4, tm), I32)
    wt_o[0:1, :] = p1 / (p1 + p2) * p_top
    wt_o[1:2, :] = p2 / (p1 + p2) * p_top
    wt_o[2:SUBLANES, :] = jnp.zeros((SUBLANES - 2, tm), F32)
    cnt_o[...] = carry[...]


def _router(x1, w_rt, b_rt, tm):
    n = x1.shape[0]
    col = pl.BlockSpec((SUBLANES, tm), lambda i: (0, i))
    return pl.pallas_call(
        functools.partial(_router_kernel, tm=tm),
        grid=(n // tm,),
        in_specs=[pl.BlockSpec((tm, D_MODEL), lambda i: (i, 0)), _full(w_rt.shape), _full(b_rt.shape)],
        out_specs=[col, col, _full((N_EXPERTS, 1))],
        out_shape=[jax.ShapeDtypeStruct((SUBLANES, n), I32), jax.ShapeDtypeStruct((SUBLANES, n), F32),
                   jax.ShapeDtypeStruct((N_EXPERTS, 1), F32)],
        scratch_shapes=[pltpu.VMEM((N_EXPERTS, 1), F32)],
        compiler_params=_params("arbitrary"),
        name="router",
    )(x1, w_rt, b_rt)


def _dispatch_kernel(dest_ref, x_ref, xs_in, xs_out, sem, *, tm, n):
    del xs_in
    i = pl.program_id(0)

    def row_copy(r, k):
        return pltpu.make_async_copy(x_ref.at[pl.ds(r, 1)], xs_out.at[pl.ds(dest_ref[k * n + i * tm + r], 1)], sem)

    def issue(r, c):
        row_copy(r, 0).start()
        row_copy(r, 1).start()
        return c

    def drain(r, c):
        row_copy(r, 0).wait()
        row_copy(r, 1).wait()
        return c

    lax.fori_loop(0, tm, issue, 0, unroll=DMA_UNROLL)
    lax.fori_loop(0, tm, drain, 0, unroll=DMA_UNROLL)


def _dispatch(dest, x1, n_slots, tm, xs_init=None):
    n = x1.shape[0]
    if xs_init is None:
        xs_init = jnp.zeros((n_slots, D_MODEL), F32)
    grid_spec = pltpu.PrefetchScalarGridSpec(
        num_scalar_prefetch=1,
        grid=(n // tm,),
        in_specs=[pl.BlockSpec((tm, D_MODEL), lambda i, d: (i, 0)), pl.BlockSpec(memory_space=pl.ANY)],
        out_specs=pl.BlockSpec(memory_space=pl.ANY),
        scratch_shapes=[pltpu.SemaphoreType.DMA])
    return pl.pallas_call(
        functools.partial(_dispatch_kernel, tm=tm, n=n),
        grid_spec=grid_spec,
        out_shape=jax.ShapeDtypeStruct((n_slots, D_MODEL), F32),
        input_output_aliases={2: 0},
        compiler_params=_params("arbitrary"),
        name="dispatch",
    )(dest, x1, xs_init)


def _expert_kernel(be_ref, nu_ref, xs_ref, wg_ref, wu_ref, wd_ref, y_o, wg_b, wu_b, wd_b):
    i = pl.program_id(0)

    @pl.when(i < nu_ref[0])
    def _():
        @pl.when((i == 0) | (be_ref[i] != be_ref[jnp.maximum(i - 1, 0)]))
        def _():
            wg_b[...] = wg_ref[...].astype(BF16)
            wu_b[...] = wu_ref[...].astype(BF16)
            wd_b[...] = wd_ref[...].astype(BF16)

        xb = xs_ref[...].astype(BF16)
        gate = _dot(xb, wg_b[...])
        h = gate * _sigmoid(gate) * _dot(xb, wu_b[...])
        y_o[...] = _dot(h.astype(BF16), wd_b[...])

    @pl.when(i >= nu_ref[0])
    def _():
        y_o[...] = jnp.zeros_like(y_o)


def _experts(block_e, n_used, xs, w_gate, w_up, w_down, layer):
    n_blocks = xs.shape[0] // SLOT_BLOCK
    blk = pl.BlockSpec((SLOT_BLOCK, D_MODEL), lambda i, be, nu: (i, 0))
    wsel = lambda i, be, nu: (layer, be[i], 0, 0)
    grid_spec = pltpu.PrefetchScalarGridSpec(
        num_scalar_prefetch=2,
        grid=(n_blocks,),
        in_specs=[blk,
                  pl.BlockSpec((None, None, D_MODEL, D_EXPERT), wsel),
                  pl.BlockSpec((None, None, D_MODEL, D_EXPERT), wsel),
                  pl.BlockSpec((None, None, D_EXPERT, D_MODEL), wsel)],
        out_specs=blk,
        scratch_shapes=[pltpu.VMEM((D_MODEL, D_EXPERT), BF16), pltpu.VMEM((D_MODEL, D_EXPERT), BF16),
                        pltpu.VMEM((D_EXPERT, D_MODEL), BF16)])
    return pl.pallas_call(
        _expert_kernel,
        grid_spec=grid_spec,
        out_shape=jax.ShapeDtypeStruct(xs.shape, F32),
        compiler_params=_params("arbitrary"),
        name="experts",
    )(block_e, n_used, xs, w_gate, w_up, w_down)


def _combine_kernel(dest_ref, x_ref, p_ref, wt_ref, wpg, wple, g_ref, b_ref, yb_ref, x2_o, buf, sem, *, tm, n):
    i = pl.program_id(0)

    def row_copy(r, k):
        return pltpu.make_async_copy(yb_ref.at[pl.ds(dest_ref[k * n + i * tm + r], 1)],
                                     buf.at[k, pl.ds(r, 1)], sem.at[k])

    def issue(r, c):
        row_copy(r, 0).start()
        row_copy(r, 1).start()
        return c

    def drain(r, c):
        row_copy(r, 0).wait()
        row_copy(r, 1).wait()
        return c

    for r in range(tm):
        issue(r, 0)
    x = x_ref[...]
    ple = _sigmoid(_dot(x.astype(BF16), wpg[...])) * _dot(p_ref[...].astype(BF16), wple[...])
    lax.fori_loop(0, tm, drain, 0, unroll=DMA_UNROLL)
    wt = wt_ref[...]
    y = wt[:, 0:1] * buf[0] + wt[:, 1:2] * buf[1]
    x2_o[...] = _layer_norm(DEEPNORM_ALPHA * x + y + ple, g_ref[...], b_ref[...])


def _combine(dest, x1, p, wt, wpg, wple, g, b, yb, tm):
    n = x1.shape[0]
    row = lambda i, d: (i, 0)
    grid_spec = pltpu.PrefetchScalarGridSpec(
        num_scalar_prefetch=1,
        grid=(n // tm,),
        in_specs=[pl.BlockSpec((tm, D_MODEL), row), pl.BlockSpec((tm, PLE_DIM), row),
                  pl.BlockSpec((tm, SUBLANES), row)]
        + [pl.BlockSpec(a.shape, lambda i, d: (0, 0)) for a in (wpg, wple, g, b)]
        + [pl.BlockSpec(memory_space=pl.ANY)],
        out_specs=pl.BlockSpec((tm, D_MODEL), row),
        scratch_shapes=[pltpu.VMEM((2, tm, D_MODEL), F32), pltpu.SemaphoreType.DMA((2,))])
    return pl.pallas_call(
        functools.partial(_combine_kernel, tm=tm, n=n),
        grid_spec=grid_spec,
        out_shape=jax.ShapeDtypeStruct((n, D_MODEL), F32),
        compiler_params=_params("arbitrary"),
        name="combine",
    )(dest, x1, p, wt, wpg, wple, g, b, yb)


def _tile(n, pref):
    return pref if n % pref == 0 else n


def _rope_tables(pos):
    def tab(half, reps):
        freq = ROPE_THETA ** (-jnp.arange(half, dtype=F32) / half)
        ang = pos.astype(F32)[:, None] * freq[None, :]
        cos, sin = jnp.cos(ang), jnp.sin(ang)
        return jnp.tile(jnp.concatenate([cos, cos], -1), (1, reps)), jnp.tile(jnp.concatenate([-sin, sin], -1), (1, reps))
    return tab(HEAD_DIM // 2, 1) + tab(IDX_DIM // 2, 2)


def _split_w_in(w_in):
    sizes = (N_HEADS * HEAD_DIM, N_KV_HEADS * HEAD_DIM, N_KV_HEADS * HEAD_DIM, N_IDX_HEADS * IDX_DIM,
             N_IDX_HEADS, IDX_DIM, D_MODEL, D_MODEL, D_MODEL, 3 * D_MODEL)
    parts, o = [], 0
    for s in sizes:
        parts.append(w_in[:, o:o + s].astype(BF16))
        o += s
    wq, wk, wv, wiq, wiw, wik, wxr, wgr, wxq, wgz = parts
    wiw = jnp.pad(wiw, ((0, 0), (0, LANES - N_IDX_HEADS)))
    wik = jnp.concatenate([wik, wik], axis=1)
    return (wq, wk, wv, wiq, wiw, wik), (wxr, wgr, wxq), wgz


def _ffn(x1, p, lw, xs_init=None):
    n = x1.shape[0]
    tm = _tile(n, 512)
    ei, wt, counts = _router(x1, lw["w_rt"], lw["b_rt"], tm)
    counts = counts[:, 0].astype(I32)
    padded = (counts + SLOT_BLOCK - 1) // SLOT_BLOCK * SLOT_BLOCK
    pad_end = jnp.cumsum(padded)
    pad_start = pad_end - padded
    experts = jnp.arange(N_EXPERTS, dtype=I32)[:, None, None]
    start = jnp.sum(jnp.where(ei[None, 0:2] == experts, pad_start[:, None, None], 0), axis=0)
    dest = (start + ei[2:4]).reshape(-1)
    n_blocks = -(-2 * n // SLOT_BLOCK) + N_EXPERTS
    blk_start = jnp.arange(n_blocks, dtype=I32) * SLOT_BLOCK
    block_e = jnp.minimum(jnp.sum((pad_end[None, :] <= blk_start[:, None]).astype(I32), axis=1), N_EXPERTS - 1)
    n_used = (pad_end[-1:] // SLOT_BLOCK).astype(I32)
    xs = _dispatch(dest, x1, n_blocks * SLOT_BLOCK, tm, xs_init)
    yb = _experts(block_e, n_used, xs, lw["w_gate"], lw["w_up"], lw["w_down"], lw["layer"])
    tc = _tile(n, 256)
    return _combine(dest, x1, p, wt.T, lw["w_ple_gate"], lw["w_ple"], lw["ln2_g"], lw["ln2_b"], yb, tc), xs


def kernel(x_prompt, x_sample, p_prompt, p_sample, cache_k, cache_v, cache_idx, state_lru_h, state_lru_conv, state_pool, page_table, w_in, w_out, lru_conv_w, lru_conv_b, lru_wa, lru_ba, lru_wi, lru_bi, lru_lambda, pool_w, pool_scale, ln1_g, ln1_b, w_router_group, b_router_group, w_router_expert, b_router_expert, w_exp_gate, w_exp_up, w_exp_down, w_ple, w_ple_gate, ln2_g, ln2_b):
    bp, tp = x_prompt.shape[:2]
    bs, ts = x_sample.shape[:2]
    assert ts == 1, "the sample group decodes one token per sequence"
    depth = w_in.shape[0]
    past = page_table.shape[1] * PAGE_SIZE
    n_p = bp * tp
    tm_p = _tile(tp, 512)
    tq = _tile(tp, DSA_Q_BLOCK)
    y_att_p = slots_p = slots_s = None
    tabs_p = _rope_tables(jnp.arange(tp))
    tabs_s = _rope_tables(jnp.full((bs,), past, I32))
    cache_idx_t = jnp.swapaxes(cache_idx, 2, 3)

    xp = x_prompt.reshape(n_p, D_MODEL)
    xs = x_sample.reshape(bs, D_MODEL)
    outs = [[] for _ in range(12)]
    row2 = lambda a: a.reshape(1, -1)
    for i in range(depth):
        w_attn, w_mix, wgz = _split_w_in(w_in[i])
        mix_w = w_mix + (lru_conv_w[i], row2(lru_conv_b[i]), lru_wa[i].astype(BF16), row2(lru_ba[i]),
                         lru_wi[i].astype(BF16), row2(lru_bi[i]), row2(lru_lambda[i]),
                         pool_w[i].astype(BF16), row2(pool_scale[i]))
        wout = w_out[i].astype(BF16)
        g1, b1 = row2(ln1_g[i]), row2(ln1_b[i])
        w_rt = jnp.concatenate([w_router_expert[i].T, w_router_group[i].T,
                                jnp.zeros((ROUTER_ROWS - N_EXPERTS - N_EXPERT_GROUPS, D_MODEL), F32)], 0)
        b_rt = jnp.concatenate([b_router_expert[i], b_router_group[i],
                                jnp.zeros((ROUTER_ROWS - N_EXPERTS - N_EXPERT_GROUPS,), F32)])[:, None]
        lw = dict(w_rt=w_rt, b_rt=b_rt, w_gate=w_exp_gate, w_up=w_exp_up, w_down=w_exp_down, layer=i,
                  w_ple_gate=w_ple_gate[i].astype(BF16),
                  w_ple=w_ple[i].astype(BF16), ln2_g=row2(ln2_g[i]), ln2_b=row2(ln2_b[i]))

        q, k, v, kb, vb, iq, iw, ik = _proj_attn(xp, w_attn, tabs_p, tm_p, tp // tm_p)
        y_att_p = _dsa_prompt(q, kb, vb, iq, iw, ik, bp, tp, tq, y_att_p)
        y_lru, y_pool, h_new, conv_new, pool_new = _mix_prompt(xp.reshape(bp, tp, D_MODEL), mix_w, bp, tp)
        x1 = _merge(xp, y_att_p, y_lru.reshape(n_p, D_MODEL), y_pool.reshape(n_p, D_MODEL), wgz, wout, g1, b1,
                    _tile(n_p, 256))
        xp, slots_p = _ffn(x1, p_prompt[i].reshape(n_p, PLE_DIM), lw, slots_p)
        outs[0].append(k.reshape(bp, tp, N_KV_HEADS, HEAD_DIM))
        outs[1].append(v.reshape(bp, tp, N_KV_HEADS, HEAD_DIM))
        outs[2].append(ik[:, :IDX_DIM].reshape(bp, tp, IDX_DIM))
        outs[3].append(h_new)
        outs[4].append(conv_new[:, SUBLANES - (CONV_WIDTH - 1):])
        outs[5].append(pool_new[:, 2 * SUBLANES - POOL_BUF:])

        q, k, v, _, _, iq, iw, ik = _proj_attn(xs, w_attn, tabs_s, bs, 1)
        sel_idx = _dsa_sample_select(page_table, iq, iw, ik[:, :IDX_DIM], cache_idx_t, i)
        y_att = _dsa_sample_attend(sel_idx[:, :, 0], page_table, q, k, v, cache_k, cache_v, i)
        y_lru, y_pool, h_new, xr, xq = _mix_sample(xs, mix_w, state_lru_conv[i], state_lru_h[i], state_pool[i], past)
        x1 = _merge(xs, y_att, y_lru, y_pool, wgz, wout, g1, b1, bs)
        xs, slots_s = _ffn(x1, p_sample[i].reshape(bs, PLE_DIM), lw, slots_s)
        outs[6].append(k.reshape(bs, ts, N_KV_HEADS, HEAD_DIM))
        outs[7].append(v.reshape(bs, ts, N_KV_HEADS, HEAD_DIM))
        outs[8].append(ik[:, :IDX_DIM].reshape(bs, ts, IDX_DIM))
        outs[9].append(h_new)
        outs[10].append(jnp.concatenate([state_lru_conv[i][:, 1:], xr[:, None]], 1))
        outs[11].append(jnp.concatenate([state_pool[i][:, 1:], xq[:, None]], 1))

    return (xp.reshape(bp, tp, D_MODEL), xs.reshape(bs, ts, D_MODEL)) + tuple(jnp.stack(o) for o in outs)
```

```python
import functools

import jax
import jax.numpy as jnp
import numpy as np
from jax import lax
from jax.experimental import pallas as pl
from jax.experimental.pallas import tpu as pltpu

F32 = jnp.float32
BF16 = jnp.bfloat16
I32 = jnp.int32

D_MODEL = 1024
N_HEADS = 8
HEAD_DIM = 128
N_KV_HEADS = 4
KV_GROUP = N_HEADS // N_KV_HEADS
N_IDX_HEADS = 8
IDX_DIM = 64
INDEX_SCALE = (IDX_DIM * N_IDX_HEADS) ** -0.5
TOPK_MAX = 256
ROPE_THETA = 10000.0
PAGE_SIZE = 128
N_LRU_BLOCKS = 8
LRU_BLOCK = D_MODEL // N_LRU_BLOCKS
CONV_WIDTH = 4
LRU_C = 8.0
POOL_WINDOWS = (2, 4, 8, 16)
POOL_GROUP = D_MODEL // len(POOL_WINDOWS)
POOL_BUF = max(POOL_WINDOWS) - 1
N_EXPERT_GROUPS = 4
EXPERTS_PER_GROUP = 8
N_EXPERTS = N_EXPERT_GROUPS * EXPERTS_PER_GROUP
D_EXPERT = 512
PLE_DIM = 256
LN_EPS = 1e-5
DEPTH = 2
DEEPNORM_ALPHA = (2 * DEPTH) ** 0.25
ATTN_SCALE = HEAD_DIM ** -0.5

LANES = 128
SUBLANES = 8
SLOT_BLOCK = 512
DSA_EXTENTS = 8
DSA_Q_BLOCK = 256
RADIX4_MAX_ELEMS = 256 * 1024
SEQ_GROUP = 8
MIX_SEQS = 8
MIX_STEPS = 64
DMA_UNROLL = 8
VMEM_LIMIT = 48 * 1024 * 1024
INT_MIN = -2 ** 31
KEY_NEG_INF = INT_MIN + 0x7FFFFF

_NT = (((1,), (1,)), ((), ()))


def _dot(a, b):
    return jnp.dot(a, b, preferred_element_type=F32)


def _dot_nt(a, b):
    return lax.dot_general(a, b, _NT, preferred_element_type=F32)


def _sigmoid(x):
    return 1.0 / (1.0 + jnp.exp(-x))


def _gelu_tanh(x):
    c = np.float32(np.sqrt(2.0 / np.pi))
    return x * (0.5 * (1.0 + jnp.tanh(c * (x + 0.044715 * (x * x * x)))))


def _softplus(x):
    return jnp.maximum(x, 0.0) + jnp.log1p(jnp.exp(-jnp.abs(x)))


def _layer_norm(r, g, b):
    mu = jnp.mean(r, axis=-1, keepdims=True)
    c = r - mu
    var = jnp.mean(c * c, axis=-1, keepdims=True)
    return c * lax.rsqrt(var + LN_EPS) * g + b


def _sort_key(x):
    bits = pltpu.bitcast(jnp.where(x == 0.0, 0.0, x), I32)
    return jnp.where(bits >= 0, bits, bits ^ 0x7FFFFFFF)


def _params(*sem):
    return pltpu.CompilerParams(dimension_semantics=sem, vmem_limit_bytes=VMEM_LIMIT)


def _full(shape):
    n = len(shape)
    return pl.BlockSpec(shape, lambda *_: (0,) * n)


def _proj_attn_kernel(x_ref, wq, wk, wv, wiq, wiw, wik, cos_ref, sin_ref, cosi_ref, sini_ref,
                      q_o, k_o, v_o, kb_o, vb_o, iq_o, iw_o, ik_o):
    xb = x_ref[...].astype(BF16)
    cos, sin = cos_ref[...], sin_ref[...]
    cosi, sini = cosi_ref[...], sini_ref[...]
    lane = lax.broadcasted_iota(I32, cos.shape, 1)
    low_half = (lane & (IDX_DIM - 1)) < (IDX_DIM // 2)

    def rope128(z):
        return z * cos + pltpu.roll(z, HEAD_DIM // 2, 1) * sin

    def rope64(z):
        partner = jnp.where(low_half, pltpu.roll(z, LANES - IDX_DIM // 2, 1), pltpu.roll(z, IDX_DIM // 2, 1))
        return z * cosi + partner * sini

    q = _dot(xb, wq[...])
    for h in range(N_HEADS):
        sl = slice(h * LANES, (h + 1) * LANES)
        q_o[:, sl] = (rope128(q[:, sl]) * ATTN_SCALE).astype(BF16)
    k = _dot(xb, wk[...])
    for c in range(N_KV_HEADS):
        sl = slice(c * LANES, (c + 1) * LANES)
        kr = rope128(k[:, sl])
        k_o[:, sl] = kr
        kb_o[:, sl] = kr.astype(BF16)
    v = _dot(xb, wv[...])
    v_o[...] = v
    vb_o[...] = v.astype(BF16)
    iq = _dot(xb, wiq[...])
    for j in range(N_IDX_HEADS * IDX_DIM // LANES):
        sl = slice(j * LANES, (j + 1) * LANES)
        iq_o[:, sl] = rope64(iq[:, sl])
    iw_o[...] = _dot(xb, wiw[...])
    ik_o[...] = rope64(_dot(xb, wik[...]))


def _proj_attn(x, w, tabs, tm, n_tab_blocks):
    n = x.shape[0]
    cos, sin, cosi, sini = tabs
    row = lambda i: (i, 0)
    tab = lambda i: (i % n_tab_blocks, 0)
    wspec = lambda a: _full(a.shape)
    outs = [
        ((n, 1024), BF16), ((n, 512), F32), ((n, 512), F32), ((n, 512), BF16), ((n, 512), BF16),
        ((n, 512), F32), ((n, LANES), F32), ((n, LANES), F32)]
    return pl.pallas_call(
        _proj_attn_kernel,
        grid=(n // tm,),
        in_specs=[pl.BlockSpec((tm, D_MODEL), row)] + [wspec(a) for a in w]
        + [pl.BlockSpec((tm, LANES), tab)] * 4,
        out_specs=[pl.BlockSpec((tm, s[1]), row) for s, _ in outs],
        out_shape=[jax.ShapeDtypeStruct(s, d) for s, d in outs],
        compiler_params=_params("parallel"),
        name="proj_attn",
    )(x, *w, cos, sin, cosi, sini)


def _count_rows(mask):
    return jnp.sum(jnp.where(mask, 1.0, 0.0), axis=1, keepdims=True)


def _dsa_prompt_kernel(q_ref, kb_ref, vb_ref, iq_ref, iw_ref, ik_ref, *rest, tq, t_len, q0, topk, idx_bits):
    o_ref, p_sc = rest[-2:]
    qi = q0 + pl.program_id(1)
    ikb = ik_ref[...].astype(BF16)
    iq = iq_ref[...]
    iw = iw_ref[...] * INDEX_SCALE
    lane = lax.broadcasted_iota(I32, (tq, LANES), 1)
    score = jnp.zeros((tq, t_len), F32)

    for h in range(N_IDX_HEADS):
        chunk = iq[:, (h // 2) * LANES:(h // 2 + 1) * LANES]
        keep = (lane < IDX_DIM) if h % 2 == 0 else (lane >= IDX_DIM)
        s = _dot_nt(jnp.where(keep, chunk, 0.0).astype(BF16), ikb)
        score = score + jnp.maximum(s, 0.0) * iw[:, h:h + 1]
    qpos = qi * tq + lax.broadcasted_iota(I32, (tq, 1), 0)
    kpos = lax.broadcasted_iota(I32, (tq, t_len), 1)
    causal = kpos <= qpos
    key = _sort_key(jnp.where(causal, score, -jnp.inf))

    if tq * t_len <= RADIX4_MAX_ELEMS:
        def thr_body(b, t):
            step = lax.shift_left(jnp.int32(1), 30 - 2 * b)
            cands = [t + step, t + 2 * step, t + 3 * step]
            ok = [_count_rows(key >= c) >= topk for c in cands]
            return jnp.where(ok[2], cands[2], jnp.where(ok[1], cands[1], jnp.where(ok[0], cands[0], t)))
        n_steps = 16
    else:
        def thr_body(b, t):
            cand = t + lax.shift_left(jnp.int32(1), 31 - b)
            return jnp.where(_count_rows(key >= cand) >= topk, cand, t)
        n_steps = 32

    thr = lax.fori_loop(0, n_steps, thr_body, jnp.full((tq, 1), INT_MIN, I32))
    gt = key > thr
    eq = key == thr
    need = topk - _count_rows(gt)
    tie = (_count_rows(eq) > need) & (thr > KEY_NEG_INF)
    p_sc[...] = jnp.full((tq, 1), t_len, I32)

    @pl.when(jnp.max(jnp.where(tie, 1.0, 0.0)) > 0.0)
    def _():
        def pos_body(b, p):
            cand = p + lax.shift_left(jnp.int32(1), idx_bits - 1 - b)
            return jnp.where(_count_rows(eq & (kpos < cand)) < need, cand, p)
        p_sc[...] = lax.fori_loop(0, idx_bits, pos_body, jnp.zeros((tq, 1), I32))

    sel = causal & (gt | (eq & (kpos <= p_sc[...])))
    bias = jnp.where(sel, 0.0, -jnp.inf)
    for c in range(N_KV_HEADS):
        kc = kb_ref[:, c * LANES:(c + 1) * LANES]
        vc = vb_ref[:, c * LANES:(c + 1) * LANES]
        vx = jnp.concatenate([vc, jnp.ones_like(vc)], axis=1)
        for g in range(KV_GROUP):
            h = c * KV_GROUP + g
            s = _dot_nt(q_ref[:, h * LANES:(h + 1) * LANES], kc) + bias
            m = jnp.max(s, axis=1, keepdims=True)
            pv = _dot(jnp.exp(s - m).astype(BF16), vx)
            o_ref[:, h * LANES:(h + 1) * LANES] = pv[:, 0:LANES] / pv[:, LANES:LANES + 1]


def _dsa_prompt(q, kb, vb, iq, iw, ik, batch, t_len, tq, y_init=None):
    topk = min(TOPK_MAX, t_len // 4)
    nq = t_len // tq
    n_var = max(d for d in range(1, DSA_EXTENTS + 1) if nq % d == 0)
    per = nq // n_var
    kb, vb, ik = (a.reshape(batch, t_len, a.shape[-1]) for a in (kb, vb, ik))
    y = jnp.zeros((batch * t_len, 1024), F32) if y_init is None else y_init
    for v in range(n_var):
        ext = (v + 1) * per * tq
        qrow = lambda b, i, v=v: (b * nq + v * per + i, 0)
        brow = lambda b, i: (b, 0, 0)
        in_specs = [pl.BlockSpec((tq, 1024), qrow), pl.BlockSpec((None, ext, 512), brow),
                    pl.BlockSpec((None, ext, 512), brow), pl.BlockSpec((tq, 512), qrow),
                    pl.BlockSpec((tq, LANES), qrow), pl.BlockSpec((None, ext, LANES), brow)]
        in_specs.append(pl.BlockSpec(memory_space=pl.ANY))
        y = pl.pallas_call(
            functools.partial(_dsa_prompt_kernel, tq=tq, t_len=ext, q0=v * per, topk=topk,
                              idx_bits=max(1, int(np.ceil(np.log2(ext))))),
            grid=(batch, per),
            in_specs=in_specs,
            out_specs=pl.BlockSpec((tq, 1024), qrow),
            out_shape=jax.ShapeDtypeStruct((batch * t_len, 1024), F32),
            scratch_shapes=[pltpu.VMEM((tq, 1), I32)],
            input_output_aliases={6: 0},
            compiler_params=_params("parallel", "arbitrary"),
            name="dsa_prompt",
        )(q, kb, vb, iq, iw, ik, y)
    return y


def _dsa_sample_select_kernel(pt_ref, iq_ref, iw_ref, ikn_ref, cidx_ref, idx_o,
                              ikbuf, s_sc, sem, *, layer, n_pages, topk, idx_bits, group):
    g0 = pl.program_id(0) * group
    past = n_pages * PAGE_SIZE

    def page_copy(s, j):
        slot = s % 2
        return pltpu.make_async_copy(cidx_ref.at[layer, pt_ref[(g0 + s) * n_pages + j]],
                                     ikbuf.at[slot, j], sem.at[slot])

    def issue_seq(s):
        def body(j, c):
            page_copy(s, j).start()
            return c
        lax.fori_loop(0, n_pages, body, 0, unroll=DMA_UNROLL)

    def drain_seq(s):
        def body(j, c):
            page_copy(s, j).wait()
            return c
        lax.fori_loop(0, n_pages, body, 0, unroll=DMA_UNROLL)

    issue_seq(0)

    def seq_body(s, c):
        @pl.when(s + 1 < group)
        def _():
            issue_seq(s + 1)
        drain_seq(s)
        slot = s % 2
        iq8 = iq_ref[s].astype(BF16)
        w8 = iw_ref[s] * INDEX_SCALE
        for j in range(n_pages):
            s8 = _dot(iq8, ikbuf[slot, j].astype(BF16))
            s_sc[s, j:j + 1, :] = jnp.sum(jnp.maximum(s8, 0.0) * w8, axis=0, keepdims=True)
        return c

    lax.fori_loop(0, group, seq_body, 0)

    w_all = iw_ref[...] * INDEX_SCALE
    own = jnp.sum(iq_ref[...].astype(BF16).astype(F32) * ikn_ref[...].astype(BF16).astype(F32),
                  axis=2, keepdims=True)
    own = jnp.sum(jnp.maximum(own, 0.0) * w_all, axis=1, keepdims=True)

    key = _sort_key(s_sc[...])
    key_own = _sort_key(own)
    pos = (lax.broadcasted_iota(I32, key.shape, 1) * PAGE_SIZE + lax.broadcasted_iota(I32, key.shape, 2))

    def count(mask, mask_own):
        c = jnp.sum(jnp.where(mask, 1.0, 0.0), axis=1, keepdims=True)
        return jnp.sum(c, axis=2, keepdims=True) + jnp.where(mask_own, 1.0, 0.0)

    def thr_body(i, t):
        cand = t + lax.shift_left(jnp.int32(1), 31 - i)
        return jnp.where(count(key >= cand, key_own >= cand) >= topk, cand, t)

    thr = lax.fori_loop(0, 32, thr_body, jnp.full((group, 1, 1), INT_MIN, I32))
    gt, eq = key > thr, key == thr
    gt_own, eq_own = key_own > thr, key_own == thr
    need = topk - count(gt, gt_own)

    def pos_body(i, p):
        cand = p + lax.shift_left(jnp.int32(1), idx_bits - 1 - i)
        return jnp.where(count(eq & (pos < cand), eq_own & (past < cand)) < need, cand, p)

    plast = lax.fori_loop(0, idx_bits, pos_body, jnp.zeros((group, 1, 1), I32))
    s_sc[...] = jnp.where(gt | (eq & (pos <= plast)), 1.0, 0.0)

    r_i = lax.broadcasted_iota(I32, (PAGE_SIZE, PAGE_SIZE), 0)
    c_i = lax.broadcasted_iota(I32, (PAGE_SIZE, PAGE_SIZE), 1)
    tri_incl = jnp.where(r_i <= c_i, 1.0, 0.0).astype(BF16)
    pr = lax.broadcasted_iota(I32, (n_pages, n_pages), 0)
    pc = lax.broadcasted_iota(I32, (n_pages, n_pages), 1)
    tri_pages = jnp.where(pr <= pc, 1.0, 0.0).astype(BF16)
    ones_rows = jnp.ones((SUBLANES, PAGE_SIZE), BF16)
    rank = lax.broadcasted_iota(I32, (topk, 1), 0).astype(F32)
    page_lane = lax.broadcasted_iota(I32, (topk, n_pages), 1).astype(F32)

    def compact(s, c):
        selb = s_sc[s].astype(BF16)
        within = _dot(selb, tri_incl)
        n_row = _dot_nt(ones_rows, selb)[0:1]
        end_row = _dot(jnp.broadcast_to(n_row, (SUBLANES, n_pages)).astype(BF16), tri_pages)[0:1]
        before = end_row <= rank
        page_of = jnp.sum(jnp.where(before, 1.0, 0.0), axis=1, keepdims=True)
        local = rank - jnp.sum(jnp.where(before, n_row, 0.0), axis=1, keepdims=True)
        page_cum = _dot(jnp.where(page_lane == page_of, 1.0, 0.0).astype(BF16), within.astype(BF16))
        lane_of = jnp.sum(jnp.where(page_cum <= local, 1.0, 0.0), axis=1, keepdims=True)
        idx_o[s] = jnp.minimum(page_of * PAGE_SIZE + lane_of, float(past)).astype(I32)
        return c

    lax.fori_loop(0, group, compact, 0)


def _dsa_sample_select(page_table, iq, iw, ik_new, cache_idx_t, layer):
    bd, n_pages = page_table.shape
    past = n_pages * PAGE_SIZE
    topk = min(TOPK_MAX, (past + 1) // 4)
    idx_bits = int(np.floor(np.log2(past))) + 1
    group = SEQ_GROUP if bd % SEQ_GROUP == 0 else bd
    grp = lambda g, pt: (g, 0, 0)
    grid_spec = pltpu.PrefetchScalarGridSpec(
        num_scalar_prefetch=1,
        grid=(bd // group,),
        in_specs=[pl.BlockSpec((group, N_IDX_HEADS, IDX_DIM), grp),
                  pl.BlockSpec((group, N_IDX_HEADS, 1), grp),
                  pl.BlockSpec((group, 1, IDX_DIM), grp),
                  pl.BlockSpec(memory_space=pl.ANY)],
        out_specs=pl.BlockSpec((group, topk, 1), grp),
        scratch_shapes=[pltpu.VMEM((2, n_pages, IDX_DIM, PAGE_SIZE), F32), pltpu.VMEM((group, n_pages, PAGE_SIZE), F32),
                        pltpu.SemaphoreType.DMA((2,))])
    return pl.pallas_call(
        functools.partial(_dsa_sample_select_kernel, layer=layer, n_pages=n_pages, topk=topk, idx_bits=idx_bits,
                          group=group),
        grid_spec=grid_spec,
        out_shape=jax.ShapeDtypeStruct((bd, topk, 1), I32),
        compiler_params=_params("arbitrary"),
        name="dsa_sample_select",
    )(page_table.reshape(-1), iq.reshape(bd, N_IDX_HEADS, IDX_DIM), iw[:, :N_IDX_HEADS, None],
      ik_new[:, None, :], cache_idx_t)


def _dsa_sample_attend_kernel(idx_ref, pt_ref, q_ref, kn_ref, vn_ref, idxv_ref, ck_ref, cv_ref, o_ref,
                              kbuf0, vbuf0, kbuf1, vbuf1, sem, *, layer, n_pages, topk):
    b = pl.program_id(0)
    last = pl.num_programs(0) - 1
    past = n_pages * PAGE_SIZE
    bufs = ((kbuf0, vbuf0), (kbuf1, vbuf1))

    def row_copies(seq, buf, r):
        pidx = jnp.minimum(idx_ref[seq * topk + r], past - 1)
        phys = pt_ref[seq * n_pages + pidx // PAGE_SIZE]
        off = pidx % PAGE_SIZE
        return (pltpu.make_async_copy(ck_ref.at[layer, phys, off], bufs[buf][0].at[r], sem.at[buf, 0]),
                pltpu.make_async_copy(cv_ref.at[layer, phys, off], bufs[buf][1].at[r], sem.at[buf, 1]))

    def issue_all(seq, buf):
        for r in range(topk):
            for cp in row_copies(seq, buf, r):
                cp.start()

    def drain_all(seq, buf):
        def body(r, c):
            for cp in row_copies(seq, buf, r):
                cp.wait()
            return c
        lax.fori_loop(0, topk, body, 0, unroll=DMA_UNROLL)

    @pl.when(b == 0)
    def _():
        issue_all(0, 0)

    def step(buf):
        drain_all(b, buf)
        nxt = jnp.minimum(b + 1, last)
        issue_all(nxt, 1 - buf)
        own = idxv_ref[...] >= past
        k_sel = jnp.where(own, kn_ref[...][None], bufs[buf][0][...])
        v_sel = jnp.where(own, vn_ref[...][None], bufs[buf][1][...])
        for g in range(KV_GROUP):
            s = jnp.sum(k_sel * q_ref[g][None], axis=-1, keepdims=True)
            m = jnp.max(s, axis=0, keepdims=True)
            p = jnp.exp(s - m)
            l = jnp.sum(p, axis=0)
            o_ref[g] = jnp.sum(p * v_sel, axis=0) / l

        @pl.when(b == last)
        def _():
            drain_all(nxt, 1 - buf)

    for parity in range(2):
        pl.when(b % 2 == parity)(functools.partial(step, parity))


def _dsa_sample_attend(idx, page_table, q, k_new, v_new, cache_k, cache_v, layer):
    bd, n_pages = page_table.shape
    topk = idx.shape[1]
    qg = q.astype(F32).reshape(bd, N_KV_HEADS, KV_GROUP, HEAD_DIM).transpose(0, 2, 1, 3)
    head = lambda b, *_: (b, 0, 0)
    grid_spec = pltpu.PrefetchScalarGridSpec(
        num_scalar_prefetch=2,
        grid=(bd,),
        in_specs=[pl.BlockSpec((None, KV_GROUP, N_KV_HEADS, HEAD_DIM), lambda b, *_: (b, 0, 0, 0)),
                  pl.BlockSpec((None, N_KV_HEADS, HEAD_DIM), head),
                  pl.BlockSpec((None, N_KV_HEADS, HEAD_DIM), head),
                  pl.BlockSpec((None, topk, 1, 1), lambda b, *_: (b, 0, 0, 0)),
                  pl.BlockSpec(memory_space=pl.ANY), pl.BlockSpec(memory_space=pl.ANY)],
        out_specs=pl.BlockSpec((None, KV_GROUP, N_KV_HEADS, HEAD_DIM), lambda b, *_: (b, 0, 0, 0)),
        scratch_shapes=[pltpu.VMEM((topk, N_KV_HEADS, HEAD_DIM), F32)] * 4 + [pltpu.SemaphoreType.DMA((2, 2))])
    o = pl.pallas_call(
        functools.partial(_dsa_sample_attend_kernel, layer=layer, n_pages=n_pages, topk=topk),
        grid_spec=grid_spec,
        out_shape=jax.ShapeDtypeStruct((bd, KV_GROUP, N_KV_HEADS, HEAD_DIM), F32),
        compiler_params=_params("arbitrary"),
        name="dsa_sample_attend",
    )(idx.reshape(-1), page_table.reshape(-1), qg, k_new.reshape(bd, N_KV_HEADS, HEAD_DIM),
      v_new.reshape(bd, N_KV_HEADS, HEAD_DIM), idx.reshape(bd, topk, 1, 1), cache_k, cache_v)
    return o.transpose(0, 2, 1, 3).reshape(bd, N_HEADS * HEAD_DIM)


def _lru_gates(xc, wa, ba, wi, bi, lam):
    xcb = xc.astype(BF16)
    a_parts, u_parts = [], []
    for n in range(N_LRU_BLOCKS):
        sl = slice(n * LRU_BLOCK, (n + 1) * LRU_BLOCK)
        r = _sigmoid(_dot(xcb[:, sl], wa[n]) + ba[:, sl])
        i = _sigmoid(_dot(xcb[:, sl], wi[n]) + bi[:, sl])
        log_a = -LRU_C * r * _softplus(-lam[:, sl])
        a_parts.append(jnp.exp(log_a))
        th = jnp.tanh(log_a)
        u_parts.append(jnp.sqrt(-2.0 * th / (1.0 - th)) * i * xc[:, sl])
    return a_parts, u_parts


def _pool_mix(window_sum, xq, cnt, pool_w, pool_scale, g):
    sl = slice(g * POOL_GROUP, (g + 1) * POOL_GROUP)
    pooled = window_sum / cnt - xq[:, sl]
    return _dot(pooled.astype(BF16), pool_w[g]) * pool_scale[:, sl]


def _mix_prompt_kernel(x_ref, wxr, wgr, wxq, convw, convb, wa, ba, wi, bi, lam, poolw, pscale,
                       ylru_o, ypool_o, h_o, conv_o, pool_o,
                       xr_ext, xq_ext, a_sc, u_sc, h_sc, h_carry, *, nb, tt):
    t = pl.program_id(1)
    halo_r, halo_q = SUBLANES, 2 * SUBLANES
    rows = nb * tt

    @pl.when(t == 0)
    def _():
        xr_ext[:, 0:halo_r] = jnp.zeros((nb, halo_r, D_MODEL), F32)
        xq_ext[:, 0:halo_q] = jnp.zeros((nb, halo_q, D_MODEL), F32)
        h_carry[...] = jnp.zeros_like(h_carry)

    @pl.when(t > 0)
    def _():
        xr_ext[:, 0:halo_r] = xr_ext[:, tt:tt + halo_r]
        xq_ext[:, 0:halo_q] = xq_ext[:, tt:tt + halo_q]

    xb = x_ref[...].reshape(rows, D_MODEL).astype(BF16)
    xr3 = _dot(xb, wxr[...]).reshape(nb, tt, D_MODEL)
    xq3 = _dot(xb, wxq[...]).reshape(nb, tt, D_MODEL)
    xr_ext[:, halo_r:halo_r + tt] = xr3
    xq_ext[:, halo_q:halo_q + tt] = xq3

    cw = convw[...]
    xc = convb[...] + cw[CONV_WIDTH - 1:CONV_WIDTH] * xr3
    for j in range(CONV_WIDTH - 1):
        o = halo_r - (CONV_WIDTH - 1) + j
        xc = xc + cw[j:j + 1] * xr_ext[:, o:o + tt]
    a_parts, u_parts = _lru_gates(xc.reshape(rows, D_MODEL), wa, ba[...], wi, bi[...], lam[...])
    for n in range(N_LRU_BLOCKS):
        sl = slice(n * LRU_BLOCK, (n + 1) * LRU_BLOCK)
        a_sc[:, sl] = a_parts[n]
        u_sc[:, sl] = u_parts[n]

    a3 = pltpu.einshape("btd->tbd", a_sc[...].reshape(nb, tt, D_MODEL))
    u3 = pltpu.einshape("btd->tbd", u_sc[...].reshape(nb, tt, D_MODEL))
    h = h_carry[...]
    for s in range(tt):
        h = a3[s] * h + u3[s]
        h_sc[s] = h
    h_carry[...] = h
    hs = pltpu.einshape("tbd->btd", h_sc[...]).reshape(rows, D_MODEL)
    ylru_o[...] = (hs * _gelu_tanh(_dot(xb, wgr[...]))).reshape(nb, tt, D_MODEL)

    posn = t * tt + lax.broadcasted_iota(I32, (1, tt, 1), 1)
    for g, w in enumerate(POOL_WINDOWS):
        sl = slice(g * POOL_GROUP, (g + 1) * POOL_GROUP)
        acc = xq_ext[:, :, sl]
        d = 1
        while d < w:
            acc = acc + pltpu.roll(acc, d, 1)
            d *= 2
        cnt = jnp.minimum(posn + 1, w).astype(F32)
        pooled = (acc[:, halo_q:] / cnt - xq3[:, :, sl]).reshape(rows, POOL_GROUP)
        mixed = _dot(pooled.astype(BF16), poolw[g]) * pscale[:, sl]
        ypool_o[:, :, sl] = mixed.reshape(nb, tt, POOL_GROUP)

    @pl.when(t == pl.num_programs(1) - 1)
    def _():
        h_o[...] = h
        conv_o[...] = xr_ext[:, tt:tt + halo_r]
        pool_o[...] = xq_ext[:, tt:tt + halo_q]


def _mix_prompt(x, w, batch, t_len):
    nb = _tile(batch, MIX_SEQS)
    tt = _tile(t_len, MIX_STEPS)
    blk = pl.BlockSpec((nb, tt, D_MODEL), lambda b, t: (b, t, 0))
    per_b = lambda b, t: (b, 0, 0)
    outs = [((batch, t_len, D_MODEL), F32), ((batch, t_len, D_MODEL), F32),
            ((batch, D_MODEL), F32), ((batch, SUBLANES, D_MODEL), F32), ((batch, 2 * SUBLANES, D_MODEL), F32)]
    return pl.pallas_call(
        functools.partial(_mix_prompt_kernel, nb=nb, tt=tt),
        grid=(batch // nb, t_len // tt),
        in_specs=[blk] + [_full(a.shape) for a in w],
        out_specs=[blk, blk, pl.BlockSpec((nb, D_MODEL), lambda b, t: (b, 0)),
                   pl.BlockSpec((nb, SUBLANES, D_MODEL), per_b), pl.BlockSpec((nb, 2 * SUBLANES, D_MODEL), per_b)],
        out_shape=[jax.ShapeDtypeStruct(s, d) for s, d in outs],
        scratch_shapes=[pltpu.VMEM((nb, tt + SUBLANES, D_MODEL), F32), pltpu.VMEM((nb, tt + 2 * SUBLANES, D_MODEL), F32),
                        pltpu.VMEM((nb * tt, D_MODEL), F32), pltpu.VMEM((nb * tt, D_MODEL), F32),
                        pltpu.VMEM((tt, nb, D_MODEL), F32), pltpu.VMEM((nb, D_MODEL), F32)],
        compiler_params=_params("parallel", "arbitrary"),
        name="mix_prompt",
    )(x, *w)


def _mix_sample_kernel(x_ref, wxr, wgr, wxq, convw, convb, wa, ba, wi, bi, lam, poolw, pscale,
                       conv_ref, h_ref, pool_ref, ylru_o, ypool_o, h_o, xr_o, xq_o, *, cnt_pos):
    xb = x_ref[...].astype(BF16)
    xr = _dot(xb, wxr[...])
    xq = _dot(xb, wxq[...])
    cw = convw[...]
    xc = convb[...] + cw[CONV_WIDTH - 1:CONV_WIDTH] * xr
    for j in range(CONV_WIDTH - 1):
        xc = xc + cw[j:j + 1] * conv_ref[j]
    a_parts, u_parts = _lru_gates(xc, wa, ba[...], wi, bi[...], lam[...])
    gate = _gelu_tanh(_dot(xb, wgr[...]))
    for n in range(N_LRU_BLOCKS):
        sl = slice(n * LRU_BLOCK, (n + 1) * LRU_BLOCK)
        h = a_parts[n] * h_ref[:, sl] + u_parts[n]
        h_o[:, sl] = h
        ylru_o[:, sl] = h * gate[:, sl]
    for g, w in enumerate(POOL_WINDOWS):
        sl = slice(g * POOL_GROUP, (g + 1) * POOL_GROUP)
        acc = xq[:, sl]
        for j in range(1, w):
            acc = acc + pool_ref[POOL_BUF - j, :, sl]
        ypool_o[:, sl] = _pool_mix(acc, xq, float(min(cnt_pos, w)), poolw, pscale[...], g)
    xr_o[...] = xr
    xq_o[...] = xq


def _mix_sample(x, w, conv_state, h_state, pool_state, past):
    bd = x.shape[0]
    args = (x, *w, conv_state.transpose(1, 0, 2), h_state, pool_state.transpose(1, 0, 2))
    shp = jax.ShapeDtypeStruct((bd, D_MODEL), F32)
    return pl.pallas_call(
        functools.partial(_mix_sample_kernel, cnt_pos=past + 1),
        in_specs=[_full(a.shape) for a in args],
        out_specs=[_full((bd, D_MODEL))] * 5,
        out_shape=[shp] * 5,
        grid=(1,),
        compiler_params=_params("arbitrary"),
        name="mix_sample",
    )(*args)


def _merge_kernel(x_ref, ya_ref, yl_ref, yp_ref, wgz, wout, g_ref, b_ref, x1_o):
    x = x_ref[...]
    gz = _dot(x.astype(BF16), wgz[...])
    merged = (_sigmoid(gz[:, 0:D_MODEL]) * ya_ref[...] + _sigmoid(gz[:, D_MODEL:2 * D_MODEL]) * yl_ref[...]
              + _sigmoid(gz[:, 2 * D_MODEL:3 * D_MODEL]) * yp_ref[...])
    r = DEEPNORM_ALPHA * x + _dot(merged.astype(BF16), wout[...])
    x1_o[...] = _layer_norm(r, g_ref[...], b_ref[...])


def _merge(x, ya, yl, yp, wgz, wout, g, b, tm):
    n = x.shape[0]
    row = pl.BlockSpec((tm, D_MODEL), lambda i: (i, 0))
    return pl.pallas_call(
        _merge_kernel,
        grid=(n // tm,),
        in_specs=[row] * 4 + [_full(wgz.shape), _full(wout.shape), _full(g.shape), _full(b.shape)],
        out_specs=row,
        out_shape=jax.ShapeDtypeStruct((n, D_MODEL), F32),
        compiler_params=_params("parallel"),
        name="merge",
    )(x, ya, yl, yp, wgz, wout, g, b)


ROUTER_ROWS = 40


def _first_argmax(v, n):
    m = jnp.max(v, axis=0, keepdims=True)
    rows = lax.broadcasted_iota(I32, v.shape, 0)
    return m, jnp.min(jnp.where(v == m, rows, n), axis=0, keepdims=True)


def _router_kernel(x_ref, w_ref, b_ref, ei_o, wt_o, cnt_o, carry, *, tm):
    i = pl.program_id(0)

    @pl.when(i == 0)
    def _():
        carry[...] = jnp.zeros_like(carry)

    def split(v):
        hi = v.astype(BF16)
        return hi, (v - hi.astype(F32)).astype(BF16)

    x_hi, x_lo = split(x_ref[...])
    w_hi, w_lo = split(w_ref[...])
    logits = _dot_nt(w_hi, x_hi) + (_dot_nt(w_hi, x_lo) + _dot_nt(w_lo, x_hi)) + b_ref[...]
    le = logits[0:N_EXPERTS]
    lg = logits[N_EXPERTS:N_EXPERTS + N_EXPERT_GROUPS]

    gmax, gidx = _first_argmax(lg, N_EXPERT_GROUPS)
    p_top = 1.0 / jnp.sum(jnp.exp(lg - gmax), axis=0, keepdims=True)
    le_g = jnp.zeros((EXPERTS_PER_GROUP, tm), F32)
    for gi in range(N_EXPERT_GROUPS):
        le_g = le_g + jnp.where(gidx == gi, le[gi * EXPERTS_PER_GROUP:(gi + 1) * EXPERTS_PER_GROUP], 0.0)
    m1, i1 = _first_argmax(le_g, EXPERTS_PER_GROUP)
    rows8 = lax.broadcasted_iota(I32, le_g.shape, 0)
    m2, i2 = _first_argmax(jnp.where(rows8 == i1, -jnp.inf, le_g), EXPERTS_PER_GROUP)
    z = jnp.sum(jnp.exp(le_g - m1), axis=0, keepdims=True)
    p1 = 1.0 / z
    p2 = jnp.exp(m2 - m1) / z
    e1 = gidx * EXPERTS_PER_GROUP + i1
    e2 = gidx * EXPERTS_PER_GROUP + i2

    rows = lax.broadcasted_iota(I32, (N_EXPERTS, tm), 0)
    hit1, hit2 = rows == e1, rows == e2
    onehot = jnp.where(hit1 | hit2, 1.0, 0.0)
    r_i = lax.broadcasted_iota(I32, (tm, tm), 0)
    c_i = lax.broadcasted_iota(I32, (tm, tm), 1)
    before = _dot(onehot.astype(BF16), jnp.where(r_i < c_i, 1.0, 0.0).astype(BF16)) + carry[...]
    carry[...] = carry[...] + jnp.sum(onehot, axis=1, keepdims=True)

    ei_o[0:1, :] = e1
    ei_o[1:2, :] = e2
    ei_o[2:3, :] = jnp.sum(jnp.where(hit1, before, 0.0), axis=0, keepdims=True).astype(I32)
    ei_o[3:4, :] = jnp.sum(jnp.where(hit2, before, 0.0), axis=0, keepdims=True).astype(I32)
    ei_o[4:SUBLANES, :] = jnp.zeros((SUBLANES - 4, tm), I32)
    wt_o[0:1, :] = p1 / (p1 + p2) * p_top
    wt_o[1:2, :] = p2 / (p1 + p2) * p_top
    wt_o[2:SUBLANES, :] = jnp.zeros((SUBLANES - 2, tm), F32)
    cnt_o[...] = carry[...]


def _router(x1, w_rt, b_rt, tm):
    n = x1.shape[0]
    col = pl.BlockSpec((SUBLANES, tm), lambda i: (0, i))
    return pl.pallas_call(
        functools.partial(_router_kernel, tm=tm),
        grid=(n // tm,),
        in_specs=[pl.BlockSpec((tm, D_MODEL), lambda i: (i, 0)), _full(w_rt.shape), _full(b_rt.shape)],
        out_specs=[col, col, _full((N_EXPERTS, 1))],
        out_shape=[jax.ShapeDtypeStruct((SUBLANES, n), I32), jax.ShapeDtypeStruct((SUBLANES, n), F32),
                   jax.ShapeDtypeStruct((N_EXPERTS, 1), F32)],
        scratch_shapes=[pltpu.VMEM((N_EXPERTS, 1), F32)],
        compiler_params=_params("arbitrary"),
        name="router",
    )(x1, w_rt, b_rt)


def _dispatch_kernel(dest_ref, x_ref, xs_in, xs_out, buf, sem, *, tm, n):
    del xs_in
    i = pl.program_id(0)
    buf[...] = pltpu.einshape("m(sl)->msl", x_ref[...], s=SUBLANES)

    def row_copy(r, k):
        return pltpu.make_async_copy(buf.at[r], xs_out.at[dest_ref[k * n + i * tm + r]], sem)

    def issue(r, c):
        row_copy(r, 0).start()
        row_copy(r, 1).start()
        return c

    def drain(r, c):
        row_copy(r, 0).wait()
        row_copy(r, 1).wait()
        return c

    lax.fori_loop(0, tm, issue, 0, unroll=DMA_UNROLL)
    lax.fori_loop(0, tm, drain, 0, unroll=DMA_UNROLL)


def _dispatch(dest, x1, n_slots, tm, xs_init=None):
    n = x1.shape[0]
    if xs_init is None:
        xs_init = jnp.zeros((n_slots, SUBLANES, LANES), F32)
    grid_spec = pltpu.PrefetchScalarGridSpec(
        num_scalar_prefetch=1,
        grid=(n // tm,),
        in_specs=[pl.BlockSpec((tm, D_MODEL), lambda i, d: (i, 0)), pl.BlockSpec(memory_space=pl.ANY)],
        out_specs=pl.BlockSpec(memory_space=pl.ANY),
        scratch_shapes=[pltpu.VMEM((tm, SUBLANES, LANES), F32), pltpu.SemaphoreType.DMA])
    return pl.pallas_call(
        functools.partial(_dispatch_kernel, tm=tm, n=n),
        grid_spec=grid_spec,
        out_shape=jax.ShapeDtypeStruct((n_slots, SUBLANES, LANES), F32),
        input_output_aliases={2: 0},
        compiler_params=_params("arbitrary"),
        name="dispatch",
    )(dest, x1, xs_init)


def _expert_kernel(be_ref, nu_ref, xs_ref, wg_ref, wu_ref, wd_ref, y_o, wg_b, wu_b, wd_b):
    i = pl.program_id(0)

    @pl.when(i < nu_ref[0])
    def _():
        @pl.when((i == 0) | (be_ref[i] != be_ref[jnp.maximum(i - 1, 0)]))
        def _():
            wg_b[...] = wg_ref[...].astype(BF16)
            wu_b[...] = wu_ref[...].astype(BF16)
            wd_b[...] = wd_ref[...].astype(BF16)

        xb = pltpu.einshape("msl->m(sl)", xs_ref[...]).astype(BF16)
        gate = _dot(xb, wg_b[...])
        h = gate * _sigmoid(gate) * _dot(xb, wu_b[...])
        y_o[...] = pltpu.einshape("m(sl)->msl", _dot(h.astype(BF16), wd_b[...]), s=SUBLANES)

    @pl.when(i >= nu_ref[0])
    def _():
        y_o[...] = jnp.zeros_like(y_o)


def _experts(block_e, n_used, xs, w_gate, w_up, w_down, layer):
    n_blocks = xs.shape[0] // SLOT_BLOCK
    blk = pl.BlockSpec((SLOT_BLOCK, SUBLANES, LANES), lambda i, be, nu: (i, 0, 0))
    wsel = lambda i, be, nu: (layer, be[i], 0, 0)
    grid_spec = pltpu.PrefetchScalarGridSpec(
        num_scalar_prefetch=2,
        grid=(n_blocks,),
        in_specs=[blk,
                  pl.BlockSpec((None, None, D_MODEL, D_EXPERT), wsel),
                  pl.BlockSpec((None, None, D_MODEL, D_EXPERT), wsel),
                  pl.BlockSpec((None, None, D_EXPERT, D_MODEL), wsel)],
        out_specs=blk,
        scratch_shapes=[pltpu.VMEM((D_MODEL, D_EXPERT), BF16), pltpu.VMEM((D_MODEL, D_EXPERT), BF16),
                        pltpu.VMEM((D_EXPERT, D_MODEL), BF16)])
    return pl.pallas_call(
        _expert_kernel,
        grid_spec=grid_spec,
        out_shape=jax.ShapeDtypeStruct(xs.shape, F32),
        compiler_params=_params("arbitrary"),
        name="experts",
    )(block_e, n_used, xs, w_gate, w_up, w_down)


def _combine_kernel(dest_ref, x_ref, p_ref, wt_ref, wpg, wple, g_ref, b_ref, yb_ref, x2_o, buf, sem, *, tm, n):
    i = pl.program_id(0)

    def row_copy(r, k):
        return pltpu.make_async_copy(yb_ref.at[dest_ref[k * n + i * tm + r]], buf.at[k, r], sem.at[k])

    def issue(r, c):
        row_copy(r, 0).start()
        row_copy(r, 1).start()
        return c

    def drain(r, c):
        row_copy(r, 0).wait()
        row_copy(r, 1).wait()
        return c

    for r in range(tm):
        issue(r, 0)
    x = x_ref[...]
    ple = _sigmoid(_dot(x.astype(BF16), wpg[...])) * _dot(p_ref[...].astype(BF16), wple[...])
    lax.fori_loop(0, tm, drain, 0, unroll=DMA_UNROLL)
    wt = wt_ref[...]
    rows = [pltpu.einshape("msl->m(sl)", buf[k]) for k in range(2)]
    y = wt[:, 0:1] * rows[0] + wt[:, 1:2] * rows[1]
    x2_o[...] = _layer_norm(DEEPNORM_ALPHA * x + y + ple, g_ref[...], b_ref[...])


def _combine(dest, x1, p, wt, wpg, wple, g, b, yb, tm):
    n = x1.shape[0]
    row = lambda i, d: (i, 0)
    grid_spec = pltpu.PrefetchScalarGridSpec(
        num_scalar_prefetch=1,
        grid=(n // tm,),
        in_specs=[pl.BlockSpec((tm, D_MODEL), row), pl.BlockSpec((tm, PLE_DIM), row),
                  pl.BlockSpec((tm, SUBLANES), row)]
        + [pl.BlockSpec(a.shape, lambda i, d: (0, 0)) for a in (wpg, wple, g, b)]
        + [pl.BlockSpec(memory_space=pl.ANY)],
        out_specs=pl.BlockSpec((tm, D_MODEL), row),
        scratch_shapes=[pltpu.VMEM((2, tm, SUBLANES, LANES), F32), pltpu.SemaphoreType.DMA((2,))])
    return pl.pallas_call(
        functools.partial(_combine_kernel, tm=tm, n=n),
        grid_spec=grid_spec,
        out_shape=jax.ShapeDtypeStruct((n, D_MODEL), F32),
        compiler_params=_params("arbitrary"),
        name="combine",
    )(dest, x1, p, wt, wpg, wple, g, b, yb)


def _tile(n, pref):
    return pref if n % pref == 0 else n


def _rope_tables(pos):
    def tab(half, reps):
        freq = ROPE_THETA ** (-jnp.arange(half, dtype=F32) / half)
        ang = pos.astype(F32)[:, None] * freq[None, :]
        cos, sin = jnp.cos(ang), jnp.sin(ang)
        return jnp.tile(jnp.concatenate([cos, cos], -1), (1, reps)), jnp.tile(jnp.concatenate([-sin, sin], -1), (1, reps))
    return tab(HEAD_DIM // 2, 1) + tab(IDX_DIM // 2, 2)


def _split_w_in(w_in):
    sizes = (N_HEADS * HEAD_DIM, N_KV_HEADS * HEAD_DIM, N_KV_HEADS * HEAD_DIM, N_IDX_HEADS * IDX_DIM,
             N_IDX_HEADS, IDX_DIM, D_MODEL, D_MODEL, D_MODEL, 3 * D_MODEL)
    parts, o = [], 0
    for s in sizes:
        parts.append(w_in[:, o:o + s].astype(BF16))
        o += s
    wq, wk, wv, wiq, wiw, wik, wxr, wgr, wxq, wgz = parts
    wiw = jnp.pad(wiw, ((0, 0), (0, LANES - N_IDX_HEADS)))
    wik = jnp.concatenate([wik, wik], axis=1)
    return (wq, wk, wv, wiq, wiw, wik), (wxr, wgr, wxq), wgz


def _ffn(x1, p, lw, xs_init=None):
    n = x1.shape[0]
    tm = _tile(n, 512)
    ei, wt, counts = _router(x1, lw["w_rt"], lw["b_rt"], tm)
    counts = counts[:, 0].astype(I32)
    padded = (counts + SLOT_BLOCK - 1) // SLOT_BLOCK * SLOT_BLOCK
    pad_end = jnp.cumsum(padded)
    pad_start = pad_end - padded
    experts = jnp.arange(N_EXPERTS, dtype=I32)[:, None, None]
    start = jnp.sum(jnp.where(ei[None, 0:2] == experts, pad_start[:, None, None], 0), axis=0)
    dest = (start + ei[2:4]).reshape(-1)
    n_blocks = -(-2 * n // SLOT_BLOCK) + N_EXPERTS
    blk_start = jnp.arange(n_blocks, dtype=I32) * SLOT_BLOCK
    block_e = jnp.minimum(jnp.sum((pad_end[None, :] <= blk_start[:, None]).astype(I32), axis=1), N_EXPERTS - 1)
    n_used = (pad_end[-1:] // SLOT_BLOCK).astype(I32)
    xs = _dispatch(dest, x1, n_blocks * SLOT_BLOCK, tm, xs_init)
    yb = _experts(block_e, n_used, xs, lw["w_gate"], lw["w_up"], lw["w_down"], lw["layer"])
    tc = _tile(n, 256)
    return _combine(dest, x1, p, wt.T, lw["w_ple_gate"], lw["w_ple"], lw["ln2_g"], lw["ln2_b"], yb, tc), xs


def kernel(x_prompt, x_sample, p_prompt, p_sample, cache_k, cache_v, cache_idx, state_lru_h, state_lru_conv, state_pool, page_table, w_in, w_out, lru_conv_w, lru_conv_b, lru_wa, lru_ba, lru_wi, lru_bi, lru_lambda, pool_w, pool_scale, ln1_g, ln1_b, w_router_group, b_router_group, w_router_expert, b_router_expert, w_exp_gate, w_exp_up, w_exp_down, w_ple, w_ple_gate, ln2_g, ln2_b):
    bp, tp = x_prompt.shape[:2]
    bs, ts = x_sample.shape[:2]
    assert ts == 1, "the sample group decodes one token per sequence"
    depth = w_in.shape[0]
    past = page_table.shape[1] * PAGE_SIZE
    n_p = bp * tp
    tm_p = _tile(tp, 512)
    tq = _tile(tp, DSA_Q_BLOCK)
    y_att_p = slots_p = slots_s = None
    tabs_p = _rope_tables(jnp.arange(tp))
    tabs_s = _rope_tables(jnp.full((bs,), past, I32))
    cache_idx_t = jnp.swapaxes(cache_idx, 2, 3)

    xp = x_prompt.reshape(n_p, D_MODEL)
    xs = x_sample.reshape(bs, D_MODEL)
    outs = [[] for _ in range(12)]
    row2 = lambda a: a.reshape(1, -1)
    for i in range(depth):
        w_attn, w_mix, wgz = _split_w_in(w_in[i])
        mix_w = w_mix + (lru_conv_w[i], row2(lru_conv_b[i]), lru_wa[i].astype(BF16), row2(lru_ba[i]),
                         lru_wi[i].astype(BF16), row2(lru_bi[i]), row2(lru_lambda[i]),
                         pool_w[i].astype(BF16), row2(pool_scale[i]))
        wout = w_out[i].astype(BF16)
        g1, b1 = row2(ln1_g[i]), row2(ln1_b[i])
        w_rt = jnp.concatenate([w_router_expert[i].T, w_router_group[i].T,
                                jnp.zeros((ROUTER_ROWS - N_EXPERTS - N_EXPERT_GROUPS, D_MODEL), F32)], 0)
        b_rt = jnp.concatenate([b_router_expert[i], b_router_group[i],
                                jnp.zeros((ROUTER_ROWS - N_EXPERTS - N_EXPERT_GROUPS,), F32)])[:, None]
        lw = dict(w_rt=w_rt, b_rt=b_rt, w_gate=w_exp_gate, w_up=w_exp_up, w_down=w_exp_down, layer=i,
                  w_ple_gate=w_ple_gate[i].astype(BF16),
                  w_ple=w_ple[i].astype(BF16), ln2_g=row2(ln2_g[i]), ln2_b=row2(ln2_b[i]))

        q, k, v, kb, vb, iq, iw, ik = _proj_attn(xp, w_attn, tabs_p, tm_p, tp // tm_p)
        y_att_p = _dsa_prompt(q, kb, vb, iq, iw, ik, bp, tp, tq, y_att_p)
        y_lru, y_pool, h_new, conv_new, pool_new = _mix_prompt(xp.reshape(bp, tp, D_MODEL), mix_w, bp, tp)
        x1 = _merge(xp, y_att_p, y_lru.reshape(n_p, D_MODEL), y_pool.reshape(n_p, D_MODEL), wgz, wout, g1, b1,
                    _tile(n_p, 256))
        xp, slots_p = _ffn(x1, p_prompt[i].reshape(n_p, PLE_DIM), lw, slots_p)
        outs[0].append(k.reshape(bp, tp, N_KV_HEADS, HEAD_DIM))
        outs[1].append(v.reshape(bp, tp, N_KV_HEADS, HEAD_DIM))
        outs[2].append(ik[:, :IDX_DIM].reshape(bp, tp, IDX_DIM))
        outs[3].append(h_new)
        outs[4].append(conv_new[:, SUBLANES - (CONV_WIDTH - 1):])
        outs[5].append(pool_new[:, 2 * SUBLANES - POOL_BUF:])

        q, k, v, _, _, iq, iw, ik = _proj_attn(xs, w_attn, tabs_s, bs, 1)
        sel_idx = _dsa_sample_select(page_table, iq, iw, ik[:, :IDX_DIM], cache_idx_t, i)
        y_att = _dsa_sample_attend(sel_idx[:, :, 0], page_table, q, k, v, cache_k, cache_v, i)
        y_lru, y_pool, h_new, xr, xq = _mix_sample(xs, mix_w, state_lru_conv[i], state_lru_h[i], state_pool[i], past)
        x1 = _merge(xs, y_att, y_lru, y_pool, wgz, wout, g1, b1, bs)
        xs, slots_s = _ffn(x1, p_sample[i].reshape(bs, PLE_DIM), lw, slots_s)
        outs[6].append(k.reshape(bs, ts, N_KV_HEADS, HEAD_DIM))
        outs[7].append(v.reshape(bs, ts, N_KV_HEADS, HEAD_DIM))
        outs[8].append(ik[:, :IDX_DIM].reshape(bs, ts, IDX_DIM))
        outs[9].append(h_new)
        outs[10].append(jnp.concatenate([state_lru_conv[i][:, 1:], xr[:, None]], 1))
        outs[11].append(jnp.concatenate([state_pool[i][:, 1:], xq[:, None]], 1))

    return (xp.reshape(bp, tp, D_MODEL), xs.reshape(bs, ts, D_MODEL)) + tuple(jnp.stack(o) for o in outs)
```
